```python
import math
import jax
import jax.numpy as jnp
from jax import lax
import numpy as np


D_MODEL = 1024
BATCH = 16
SEQ = 2048
DEPTH = 2

CHUNK = 64
D_PLE = 256
N_BRANCH = 4
D_BRANCH = D_MODEL // 4
D_FF = 4 * D_MODEL
EPS = 1e-6
NEG_INF = -1e30

HG_HEADS = 4
HG_DK = D_BRANCH // HG_HEADS
HG_DV = D_BRANCH // HG_HEADS

SSD_HEADS = 4
SSD_HEADDIM = D_BRANCH // SSD_HEADS
SSD_GROUPS = 2
SSD_DSTATE = 64
SSD_CONV = 4
SSD_INNER = D_BRANCH
SSD_XBC = SSD_INNER + 2 * SSD_GROUPS * SSD_DSTATE

S5_GROUP_CH = 16
S5_GROUPS = D_BRANCH // S5_GROUP_CH
S5_STATE = 64

ATT_HEADS = 4
ATT_HEADDIM = D_BRANCH // ATT_HEADS
ATT_LEFT_CHUNKS = 8
ATT_BAND = (ATT_LEFT_CHUNKS + 1) * CHUNK
ATT_MAX_REL = 128

SPLIT_SIZES = (
    4 * D_BRANCH,
    SSD_INNER + SSD_XBC + SSD_HEADS,
    D_BRANCH,
    3 * D_BRANCH,
    N_BRANCH * D_MODEL,
)
N_IN = sum(SPLIT_SIZES)

kernel_name = 'hybrid_gated_branch_streaming_encoder'


def rmsnorm(x, g):
    xf = x.astype(jnp.float32)
    y = xf * lax.rsqrt(jnp.mean(xf * xf, axis=-1, keepdims=True) + EPS)
    return (y * g.astype(jnp.float32)).astype(x.dtype)


def split_cols(t, sizes):
    offsets = np.cumsum(np.asarray(sizes))[:-1].tolist()
    return jnp.split(t, offsets, axis=-1)


def to_chunks(t, n_heads, d):
    b_, s_ = t.shape[:2]
    return t.reshape(b_, s_ // CHUNK, CHUNK, n_heads, d).transpose(0, 3, 1, 2, 4)


def hgrn2_mixer(q, f_logit, i_in, g, lb, o_gain):
    f32 = jnp.float32
    b_, s_ = q.shape[:2]
    z = f_logit.astype(f32)
    lbf = lb.astype(f32)
    log_f = jnp.logaddexp(jnp.log(lbf), jnp.log1p(-lbf) + jax.nn.log_sigmoid(z))
    k = (1.0 - lbf) * jax.nn.sigmoid(-z)

    def seq_major(t, d):
        return to_chunks(t.astype(f32), HG_HEADS, d).transpose(2, 0, 1, 3, 4)

    xs = (seq_major(q, HG_DK), seq_major(log_f, HG_DK), seq_major(k, HG_DK), seq_major(i_in, HG_DV))
    pos = jnp.arange(CHUNK)
    causal = (pos[:, None] >= pos[None, :])[:, :, None]

    def step(state, inp):
        q_c, lf_c, k_c, v_c = inp
        b = jnp.cumsum(lf_c, axis=2)
        diff = b[:, :, :, None, :] - b[:, :, None, :, :]
        decay = jnp.where(causal, jnp.exp(jnp.where(causal, diff, 0.0)), 0.0)
        scores = jnp.einsum('bhtk,bhjk,bhtjk->bhtj', q_c, k_c, decay)
        o_c = (jnp.einsum('bhtj,bhjv->bhtv', scores, v_c)
               + jnp.einsum('bhtk,bhkv->bhtv', q_c * jnp.exp(b), state))
        b_end = b[:, :, -1:, :]
        state = (jnp.exp(b_end[:, :, 0, :, None]) * state
                 + jnp.einsum('bhjk,bhjv->bhkv', k_c * jnp.exp(b_end - b), v_c))
        return state, o_c

    s0 = jnp.zeros((b_, HG_HEADS, HG_DK, HG_DV), f32)
    _, o = lax.scan(step, s0, xs)
    o = o.transpose(1, 0, 3, 2, 4).reshape(b_, s_, HG_HEADS, HG_DV)
    o = o * lax.rsqrt(jnp.mean(o * o, axis=-1, keepdims=True) + EPS) * o_gain.astype(f32).reshape(HG_HEADS, HG_DV)
    o = o.reshape(b_, s_, D_BRANCH) * jax.nn.silu(g.astype(f32))
    return o.astype(q.dtype)


def segsum(t):
    tc = jnp.cumsum(t, axis=-1)
    idx = jnp.arange(t.shape[-1])
    mask = idx[:, None] >= idx[None, :]
    return jnp.where(mask, tc[..., :, None] - tc[..., None, :], -jnp.inf)


def ssd_mixer(z, xbc, dt_raw, conv_w, conv_b, dt_bias, a_log, d_skip, norm_g):
    f32 = jnp.float32
    b_, s_ = z.shape[:2]
    nc = s_ // CHUNK
    xbc = xbc.astype(f32)
    xpad = jnp.pad(xbc, ((0, 0), (SSD_CONV - 1, 0), (0, 0)))
    conv = conv_b.astype(f32)
    for tap in range(SSD_CONV):
        conv = conv + xpad[:, tap:tap + s_, :] * conv_w[:, tap].astype(f32)
    xbc = jax.nn.silu(conv)
    xs, bm, cm = split_cols(xbc, (SSD_INNER, SSD_GROUPS * SSD_DSTATE, SSD_GROUPS * SSD_DSTATE))
    rep = SSD_HEADS // SSD_GROUPS
    xs = xs.reshape(b_, nc, CHUNK, SSD_HEADS, SSD_HEADDIM)
    bh = jnp.repeat(bm.reshape(b_, nc, CHUNK, SSD_GROUPS, SSD_DSTATE), rep, axis=3)
    ch = jnp.repeat(cm.reshape(b_, nc, CHUNK, SSD_GROUPS, SSD_DSTATE), rep, axis=3)
    dt = jax.nn.softplus(dt_raw.astype(f32) + dt_bias.astype(f32))
    a = -jnp.exp(a_log.astype(f32))
    a_dt = (dt * a).reshape(b_, nc, CHUNK, SSD_HEADS).transpose(0, 3, 1, 2)
    xdt = xs * dt.reshape(b_, nc, CHUNK, SSD_HEADS)[..., None]
    a_cum = jnp.cumsum(a_dt, axis=-1)
    l_mat = jnp.exp(segsum(a_dt))
    y_diag = jnp.einsum('bclhn,bcshn,bhcls,bcshp->bclhp', ch, bh, l_mat, xdt)
    decay_states = jnp.exp(a_cum[..., -1:] - a_cum)
    states = jnp.einsum('bclhn,bhcl,bclhp->bchpn', bh, decay_states, xdt)
    states = jnp.concatenate([jnp.zeros_like(states[:, :1]), states], axis=1)
    decay_chunk = jnp.exp(segsum(jnp.pad(a_cum[..., -1], ((0, 0), (0, 0), (1, 0)))))
    states = jnp.einsum('bhzc,bchpn->bzhpn', decay_chunk, states)[:, :-1]
    y_off = jnp.einsum('bclhn,bchpn,bhcl->bclhp', ch, states, jnp.exp(a_cum))
    y = (y_diag + y_off + d_skip.astype(f32)[:, None] * xs).reshape(b_, s_, SSD_INNER)
    y = y * jax.nn.silu(z.astype(f32))
    y = y.reshape(b_, s_, SSD_GROUPS, SSD_INNER // SSD_GROUPS)
    y = y * lax.rsqrt(jnp.mean(y * y, axis=-1, keepdims=True) + EPS)
    y = y.reshape(b_, s_, SSD_INNER) * norm_g.astype(f32)
    return y.astype(z.dtype)


def s5_combine(c1, c2):
    a1r, a1i, b1r, b1i = c1
    a2r, a2i, b2r, b2i = c2
    return (a2r * a1r - a2i * a1i,
            a2r * a1i + a2i * a1r,
            a2r * b1r - a2i * b1i + b2r,
            a2r * b1i + a2i * b1r + b2i)


def s5_mixer(u, a_re, a_im, b_re, b_im, c_re, c_im, d_skip, log_dt, w_glu):
    f32 = jnp.float32
    b_, s_ = u.shape[:2]
    a_re = a_re.astype(f32)
    a_im = a_im.astype(f32)
    step = jnp.exp(log_dt.astype(f32))[:, None]
    mag = jnp.exp(a_re * step)
    lam_re = mag * jnp.cos(a_im * step)
    lam_im = mag * jnp.sin(a_im * step)
    den = a_re * a_re + a_im * a_im
    num_re = lam_re - 1.0
    coef_re = (num_re * a_re + lam_im * a_im) / den
    coef_im = (lam_im * a_re - num_re * a_im) / den
    b_re = b_re.astype(f32)
    b_im = b_im.astype(f32)
    bb_re = coef_re[..., None] * b_re - coef_im[..., None] * b_im
    bb_im = coef_re[..., None] * b_im + coef_im[..., None] * b_re
    uf = u.astype(f32)
    ug = uf.reshape(b_, s_, S5_GROUPS, S5_GROUP_CH)
    bu_re = jnp.einsum('bsgi,gpi->bsgp', ug, bb_re)
    bu_im = jnp.einsum('bsgi,gpi->bsgp', ug, bb_im)
    lam_re_full = jnp.broadcast_to(lam_re, bu_re.shape)
    lam_im_full = jnp.broadcast_to(lam_im, bu_re.shape)
    _, _, h_re, h_im = lax.associative_scan(s5_combine, (lam_re_full, lam_im_full, bu_re, bu_im), axis=1)
    y = (jnp.einsum('bsgp,gip->bsgi', h_re, c_re.astype(f32))
         - jnp.einsum('bsgp,gip->bsgi', h_im, c_im.astype(f32)))
    y = y.reshape(b_, s_, D_BRANCH) + d_skip.astype(f32) * uf
    y = jax.nn.gelu(y)
    y = y * jax.nn.sigmoid(y @ w_glu.astype(f32))
    return y.astype(u.dtype)


def chunk_rel_attention(q, k, v, q_gain, k_gain, rel_bias):
    f32 = jnp.float32
    b_, s_ = q.shape[:2]
    nc = s_ // CHUNK
    q = rmsnorm(q.reshape(b_, s_, ATT_HEADS, ATT_HEADDIM), q_gain).reshape(b_, s_, D_BRANCH)
    k = rmsnorm(k.reshape(b_, s_, ATT_HEADS, ATT_HEADDIM), k_gain).reshape(b_, s_, D_BRANCH)
    qc = to_chunks(q, ATT_HEADS, ATT_HEADDIM)
    kc = to_chunks(k, ATT_HEADS, ATT_HEADDIM)
    vc = to_chunks(v, ATT_HEADS, ATT_HEADDIM)
    pad = ((0, 0), (0, 0), (ATT_LEFT_CHUNKS, 0), (0, 0), (0, 0))
    kp = jnp.pad(kc, pad)
    vp = jnp.pad(vc, pad)
    kband = jnp.concatenate([kp[:, :, j:j + nc] for j in range(ATT_LEFT_CHUNKS + 1)], axis=3)
    vband = jnp.concatenate([vp[:, :, j:j + nc] for j in range(ATT_LEFT_CHUNKS + 1)], axis=3)
    scores = jnp.einsum('bhcqd,bhckd->bhcqk', qc, kband).astype(f32) * (ATT_HEADDIM ** -0.5)
    qpos = jnp.arange(CHUNK)[:, None]
    kpos = jnp.arange(ATT_BAND)[None, :]
    rel = jnp.clip(qpos + ATT_LEFT_CHUNKS * CHUNK - kpos, -ATT_MAX_REL, ATT_MAX_REL) + ATT_MAX_REL
    bias = rel_bias.astype(f32)[:, rel]
    valid = (jnp.arange(nc)[:, None] - ATT_LEFT_CHUNKS + kpos // CHUNK) >= 0
    scores = jnp.where(valid[None, None, :, None, :], scores + bias[None, :, None], NEG_INF)
    probs = jax.nn.softmax(scores, axis=-1).astype(vband.dtype)
    out = jnp.einsum('bhcqk,bhckd->bhcqd', probs, vband)
    return out.transpose(0, 2, 3, 1, 4).reshape(b_, s_, D_BRANCH)


def setup_inputs(seed: int = 0) -> dict:
    key = jax.random.key(seed)
    keys = iter(jax.random.split(key, 40))
    f32 = jnp.float32
    L = DEPTH

    def normal(shape, scale):
        return jax.random.normal(next(keys), shape, f32) * scale

    def uniform(shape, lo, hi):
        return jax.random.uniform(next(keys), shape, f32, lo, hi)

    def gain(shape):
        return 1.0 + normal(shape, 0.02)

    ssd_dt = jnp.exp(uniform((L, SSD_HEADS), math.log(1e-3), math.log(1e-1)))
    s5_a_im = jnp.pi * jnp.arange(S5_STATE, dtype=f32)
    return {
        'x': normal((BATCH, SEQ, D_MODEL), 1.0),
        'p': normal((DEPTH, BATCH, SEQ, D_PLE), 1.0),
        'norm_mix': gain((L, D_MODEL)),
        'w_in': normal((L, D_MODEL, N_IN), D_MODEL ** -0.5),
        'hg_lb_logits': normal((L, HG_HEADS * HG_DK), 0.5),
        'hg_o_norm': gain((L, D_BRANCH)),
        'ssd_conv_w': normal((L, SSD_XBC, SSD_CONV), SSD_CONV ** -0.5),
        'ssd_conv_b': normal((L, SSD_XBC), 0.02),
        'ssd_dt_bias': ssd_dt + jnp.log(-jnp.expm1(-ssd_dt)),
        'ssd_A_log': jnp.log(uniform((L, SSD_HEADS), 1.0, 16.0)),
        'ssd_D': 1.0 + normal((L, SSD_HEADS), 0.1),
        'ssd_norm': gain((L, SSD_INNER)),
        's5_A_re': -0.5 + normal((L, S5_GROUPS, S5_STATE), 0.01),
        's5_A_im': s5_a_im + normal((L, S5_GROUPS, S5_STATE), 0.01),
        's5_B_re': normal((L, S5_GROUPS, S5_STATE, S5_GROUP_CH), (2 * S5_GROUP_CH) ** -0.5),
        's5_B_im': normal((L, S5_GROUPS, S5_STATE, S5_GROUP_CH), (2 * S5_GROUP_CH) ** -0.5),
        's5_C_re': normal((L, S5_GROUPS, S5_GROUP_CH, S5_STATE), (2 * S5_STATE) ** -0.5),
        's5_C_im': normal((L, S5_GROUPS, S5_GROUP_CH, S5_STATE), (2 * S5_STATE) ** -0.5),
        's5_D': normal((L, D_BRANCH), 0.5),
        's5_log_dt': uniform((L, S5_GROUPS), math.log(1e-3), math.log(1e-1)),
        's5_w_glu': normal((L, D_BRANCH, D_BRANCH), D_BRANCH ** -0.5),
        'att_q_norm': gain((L, ATT_HEADDIM)),
        'att_k_norm': gain((L, ATT_HEADDIM)),
        'att_rel_bias': normal((L, ATT_HEADS, 2 * ATT_MAX_REL + 1), 0.1),
        'w_branch': normal((L, N_BRANCH, D_BRANCH, D_MODEL), D_BRANCH ** -0.5),
        'w_out': normal((L, D_MODEL, D_MODEL), D_MODEL ** -0.5),
        'norm_ffn': gain((L, D_MODEL)),
        'w_ff1': normal((L, D_MODEL, D_FF), D_MODEL ** -0.5),
        'w_ff2': normal((L, D_FF, D_MODEL), D_FF ** -0.5),
        'w_ple': normal((L, D_PLE, D_MODEL), D_PLE ** -0.5),
        'norm_ple': gain((L, D_MODEL)),
        'w_ple_gate': normal((L, D_MODEL, D_MODEL), D_MODEL ** -0.5),
    }


def reference(x, p, norm_mix, w_in, hg_lb_logits, hg_o_norm, ssd_conv_w, ssd_conv_b, ssd_dt_bias,
              ssd_A_log, ssd_D, ssd_norm, s5_A_re, s5_A_im, s5_B_re, s5_B_im, s5_C_re, s5_C_im, s5_D,
              s5_log_dt, s5_w_glu, att_q_norm, att_k_norm, att_rel_bias, w_branch, w_out, norm_ffn,
              w_ff1, w_ff2, w_ple, norm_ple, w_ple_gate):
    b_, s_ = x.shape[:2]
    lb_all = jnp.cumsum(jax.nn.softmax(hg_lb_logits.astype(jnp.float32), axis=0), axis=0)
    lb_all = lb_all - lb_all[0:1]
    for i in range(DEPTH):
        h = rmsnorm(x, norm_mix[i])
        proj = h @ w_in[i]
        a_in, b_in, c_in, d_in, gate_in = split_cols(proj, SPLIT_SIZES)

        hq, hf, hi, hg = jnp.split(a_in, 4, axis=-1)
        y_a = hgrn2_mixer(hq, hf, hi, hg, lb_all[i], hg_o_norm[i])

        sz, sxbc, sdt = split_cols(b_in, (SSD_INNER, SSD_XBC, SSD_HEADS))
        y_b = ssd_mixer(sz, sxbc, sdt, ssd_conv_w[i], ssd_conv_b[i], ssd_dt_bias[i], ssd_A_log[i],
                        ssd_D[i], ssd_norm[i])

        y_c = s5_mixer(c_in, s5_A_re[i], s5_A_im[i], s5_B_re[i], s5_B_im[i], s5_C_re[i], s5_C_im[i],
                       s5_D[i], s5_log_dt[i], s5_w_glu[i])

        aq, ak, av = jnp.split(d_in, 3, axis=-1)
        y_d = chunk_rel_attention(aq, ak, av, att_q_norm[i], att_k_norm[i], att_rel_bias[i])

        gates = jax.nn.sigmoid(gate_in.reshape(b_, s_, N_BRANCH, D_MODEL))
        branches = (y_a, y_b, y_c, y_d)
        merged = gates[:, :, 0] * (branches[0] @ w_branch[i, 0])
        for m in range(1, N_BRANCH):
            merged = merged + gates[:, :, m] * (branches[m] @ w_branch[i, m])
        x = x + merged @ w_out[i]

        h2 = rmsnorm(x, norm_ffn[i])
        x = x + jnp.square(jax.nn.relu(h2 @ w_ff1[i])) @ w_ff2[i]

        ple_gate = jax.nn.sigmoid(rmsnorm(x, norm_ple[i]) @ w_ple_gate[i])
        x = x + (p[i] @ w_ple[i]) * ple_gate
    return x
```

```python
import functools
import math

import jax
import jax.numpy as jnp
import numpy as np
from jax import lax
from jax.experimental import pallas as pl
from jax.experimental.pallas import tpu as pltpu

F32 = jnp.float32
BF16 = jnp.bfloat16

D_MODEL = 1024
CHUNK = 64
D_PLE = 256
N_BRANCH = 4
D_BRANCH = 256
D_FF = 4096
EPS = 1e-6
NEG_INF = -1e30

HG_HEADS = 4
HG_DK = 64
SSD_HEADS = 4
SSD_HEADDIM = 64
SSD_GROUPS = 2
SSD_DSTATE = 64
SSD_CONV = 4
SSD_INNER = 256
SSD_XBC = 512
S5_GROUP_CH = 16
S5_GROUPS = 16
S5_STATE = 64
ATT_HEADS = 4
ATT_HEADDIM = 64
ATT_LEFT_CHUNKS = 8
ATT_MAX_REL = 128

COL_HG = 0
COL_XBC = 1024
COL_Z = 1536
COL_U = 1792
COL_QKV = 2048
COL_DT = 2816
N_MIX = 2944
DT_PAD = 128

S5_L = 8
S5_HALF_G = 8
S5_HALF_STATE = S5_HALF_G * S5_STATE

ATT_BAND = (ATT_LEFT_CHUNKS + 2) * CHUNK
ATT_PAD = (ATT_LEFT_CHUNKS + 1) * CHUNK

VMEM_LIMIT = 56 * 1024 * 1024


def _cparams(n_axes=1):
    return pltpu.CompilerParams(
        dimension_semantics=("arbitrary",) * n_axes, vmem_limit_bytes=VMEM_LIMIT)


def _dot(a, b):
    return jnp.dot(a, b, preferred_element_type=F32)


def _dot_nt(a, b):
    return lax.dot_general(a, b, (((1,), (1,)), ((), ())), preferred_element_type=F32)


def _dot_tn(a, b):
    return lax.dot_general(a, b, (((0,), (0,)), ((), ())), preferred_element_type=F32)


def _split3(a):
    a1 = a.astype(BF16)
    r1 = a - a1.astype(F32)
    a2 = r1.astype(BF16)
    r2 = r1 - a2.astype(F32)
    return a1, a2, r2.astype(BF16)


def _dot_exact_rhs(a, m):
    a1, a2, a3 = _split3(a)
    return _dot(a1, m) + _dot(a2, m) + _dot(a3, m)


def _dot_exact_lhs(m, a):
    a1, a2, a3 = _split3(a)
    return _dot(m, a1) + _dot(m, a2) + _dot(m, a3)


def _dot_tn_exact_rhs(a, m):
    a1, a2, a3 = _split3(a)
    return _dot_tn(a1, m) + _dot_tn(a2, m) + _dot_tn(a3, m)


def _sigmoid(x):
    return 1.0 / (1.0 + jnp.exp(-x))


def _softplus(x):
    return jnp.maximum(x, 0.0) + jnp.log1p(jnp.exp(-jnp.abs(x)))


def _rms_rows(x, g):
    return x * lax.rsqrt(jnp.mean(x * x, axis=-1, keepdims=True) + EPS) * g


def _inproj_kernel(x_ref, g_ref, w_ref, o_ref):
    h = _rms_rows(x_ref[...], g_ref[...])
    o_ref[...] = _dot(h.astype(BF16), w_ref[...])


def _inproj(x2, g, w_mix, tm=512):
    t = x2.shape[0]
    return pl.pallas_call(
        _inproj_kernel,
        grid=(t // tm,),
        in_specs=[
            pl.BlockSpec((tm, D_MODEL), lambda i: (i, 0)),
            pl.BlockSpec((1, D_MODEL), lambda i: (0, 0)),
            pl.BlockSpec((D_MODEL, N_MIX), lambda i: (0, 0)),
        ],
        out_specs=pl.BlockSpec((tm, N_MIX), lambda i: (i, 0)),
        out_shape=jax.ShapeDtypeStruct((t, N_MIX), F32),
        compiler_params=_cparams(),
        name="inproj",
    )(x2, g, w_mix)


def _merge_kernel(x_ref, ya_ref, yb_ref, yc_ref, yd_ref, g_ref, wg_ref, wb_ref, wo_ref, o_ref):
    x = x_ref[...]
    h = _rms_rows(x, g_ref[...]).astype(BF16)
    merged = None
    for m, y_ref in enumerate((ya_ref, yb_ref, yc_ref, yd_ref)):
        gate = _sigmoid(_dot(h, wg_ref[:, m * D_MODEL:(m + 1) * D_MODEL]))
        term = gate * _dot(y_ref[...].astype(BF16), wb_ref[m])
        merged = term if merged is None else merged + term
    o_ref[...] = x + _dot(merged.astype(BF16), wo_ref[...])


def _merge(x2, ya, yb, yc, yd, g, w_gate, w_branch, w_out, tm=256):
    t = x2.shape[0]
    row = lambda i: (i, 0)
    return pl.pallas_call(
        _merge_kernel,
        grid=(t // tm,),
        in_specs=[
            pl.BlockSpec((tm, D_MODEL), row),
            pl.BlockSpec((tm, D_BRANCH), row),
            pl.BlockSpec((tm, D_BRANCH), row),
            pl.BlockSpec((tm, D_BRANCH), row),
            pl.BlockSpec((tm, D_BRANCH), row),
            pl.BlockSpec((1, D_MODEL), lambda i: (0, 0)),
            pl.BlockSpec((D_MODEL, N_BRANCH * D_MODEL), lambda i: (0, 0)),
            pl.BlockSpec((N_BRANCH, D_BRANCH, D_MODEL), lambda i: (0, 0, 0)),
            pl.BlockSpec((D_MODEL, D_MODEL), lambda i: (0, 0)),
        ],
        out_specs=pl.BlockSpec((tm, D_MODEL), row),
        out_shape=jax.ShapeDtypeStruct((t, D_MODEL), F32),
        compiler_params=_cparams(),
        name="merge",
    )(x2, ya, yb, yc, yd, g, w_gate, w_branch, w_out)


def _ffn_kernel(x_ref, g_ref, w1_ref, w2_ref, o_ref):
    x = x_ref[...]
    h = _rms_rows(x, g_ref[...]).astype(BF16)
    acc = x
    tk = 1024
    for j in range(D_FF // tk):
        a = jnp.maximum(_dot(h, w1_ref[:, j * tk:(j + 1) * tk]), 0.0)
        acc = acc + _dot((a * a).astype(BF16), w2_ref[j * tk:(j + 1) * tk, :])
    o_ref[...] = acc


def _ffn(x2, g, w1, w2, tm=256):
    t = x2.shape[0]
    return pl.pallas_call(
        _ffn_kernel,
        grid=(t // tm,),
        in_specs=[
            pl.BlockSpec((tm, D_MODEL), lambda i: (i, 0)),
            pl.BlockSpec((1, D_MODEL), lambda i: (0, 0)),
            pl.BlockSpec((D_MODEL, D_FF), lambda i: (0, 0)),
            pl.BlockSpec((D_FF, D_MODEL), lambda i: (0, 0)),
        ],
        out_specs=pl.BlockSpec((tm, D_MODEL), lambda i: (i, 0)),
        out_shape=jax.ShapeDtypeStruct((t, D_MODEL), F32),
        compiler_params=_cparams(),
        name="ffn",
    )(x2, g, w1, w2)


def _ple_kernel(x_ref, p_ref, g_ref, wg_ref, wp_ref, o_ref):
    x = x_ref[...]
    h = _rms_rows(x, g_ref[...]).astype(BF16)
    gate = _sigmoid(_dot(h, wg_ref[...]))
    o_ref[...] = x + _dot(p_ref[...].astype(BF16), wp_ref[...]) * gate


def _ple(x2, p2, g, w_gate, w_ple, tm=512):
    t = x2.shape[0]
    return pl.pallas_call(
        _ple_kernel,
        grid=(t // tm,),
        in_specs=[
            pl.BlockSpec((tm, D_MODEL), lambda i: (i, 0)),
            pl.BlockSpec((tm, D_PLE), lambda i: (i, 0)),
            pl.BlockSpec((1, D_MODEL), lambda i: (0, 0)),
            pl.BlockSpec((D_MODEL, D_MODEL), lambda i: (0, 0)),
            pl.BlockSpec((D_PLE, D_MODEL), lambda i: (0, 0)),
        ],
        out_specs=pl.BlockSpec((tm, D_MODEL), lambda i: (i, 0)),
        out_shape=jax.ShapeDtypeStruct((t, D_MODEL), F32),
        compiler_params=_cparams(),
        name="ple",
    )(x2, p2, g, w_gate, w_ple)


def _tri64():
    i = np.arange(CHUNK)
    return jnp.asarray(i[:, None] >= i[None, :], BF16)


def _ones_bd(block, n=D_BRANCH):
    i = np.arange(n) // block
    return jnp.asarray(i[:, None] == i[None, :], BF16)


def _head_masks():
    lane = np.arange(D_BRANCH) // 64
    return jnp.asarray(lane[None, :] == np.arange(4)[:, None], F32)


HG_JGROUP = 8


def _hgrn_kernel(p_ref, loglb_ref, log1mlb_ref, omlb_ref, gain_ref, tri_ref, onesbd_ref,
                 bdmask_ref, ones64_ref, o_ref, s_ref, b_s, kk_s, v_s, w_s):
    seq = p_ref.shape[0]
    s_ref[...] = jnp.zeros_like(s_ref)
    rowid = lax.broadcasted_iota(jnp.int32, (CHUNK, D_BRANCH), 0)

    def chunk(c, carry):
        r0 = pl.multiple_of(c * CHUNK, CHUNK)
        rows = pl.ds(r0, CHUNK)
        q = p_ref[rows, 0:256]
        z = p_ref[rows, 256:512]
        v = p_ref[rows, 512:768]
        g = p_ref[rows, 768:1024]
        y = log1mlb_ref[...] + (jnp.minimum(z, 0.0) - jnp.log1p(jnp.exp(-jnp.abs(z))))
        a = loglb_ref[...]
        lf = jnp.maximum(a, y) + jnp.log1p(jnp.exp(-jnp.abs(a - y)))
        kk = omlb_ref[...] * _sigmoid(-z)
        b = _dot_exact_lhs(tri_ref[...], lf)
        b_s[...] = b
        kk_s[...] = kk
        v_s[...] = v

        def jgroup(jg, o_acc):
            for jj in range(HG_JGROUP):
                j = jg * HG_JGROUP + jj
                d = jnp.where(rowid >= j, b - b_s[pl.ds(j, 1), :], NEG_INF)
                w = q * jnp.exp(d) * kk_s[pl.ds(j, 1), :]
                w_s[jj * CHUNK:(jj + 1) * CHUNK, :] = w.astype(BF16)
            r = _dot(w_s[...], onesbd_ref[...])
            for jj in range(HG_JGROUP):
                j = jg * HG_JGROUP + jj
                o_acc = o_acc + r[jj * CHUNK:(jj + 1) * CHUNK, :] * v_s[pl.ds(j, 1), :]
            return o_acc

        o = lax.fori_loop(0, CHUNK // HG_JGROUP, jgroup, jnp.zeros((CHUNK, D_BRANCH), F32))

        s_old = s_ref[...]
        o = o + _dot((q * jnp.exp(b)).astype(BF16), s_old.astype(BF16))
        bend = b[CHUNK - 1:CHUNK, :]
        ka = (kk * jnp.exp(bend - b)).astype(BF16)
        upd = _dot_tn(ka, v.astype(BF16))
        dcol = jnp.exp(_dot_tn_exact_rhs(lf, ones64_ref[...]))
        s_ref[...] = dcol * s_old + upd * bdmask_ref[...]

        ms = _dot_exact_rhs(o * o, onesbd_ref[...]) * (1.0 / HG_DK)
        o_ref[rows, :] = o * lax.rsqrt(ms + EPS) * gain_ref[...] * (g * _sigmoid(g))
        return carry

    lax.fori_loop(0, seq // CHUNK, chunk, 0)


def _hgrn(proj, lb, o_gain, batch, seq):
    lb = lb.reshape(1, D_BRANCH)
    vec = pl.BlockSpec((1, D_BRANCH), lambda b: (0, 0))
    full = lambda shape: pl.BlockSpec(shape, lambda b: (0,) * len(shape))
    bd = _ones_bd(64)
    return pl.pallas_call(
        _hgrn_kernel,
        grid=(batch,),
        in_specs=[
            pl.BlockSpec((seq, 1024), lambda b: (b, COL_HG // 1024)),
            vec, vec, vec, vec,
            full((CHUNK, CHUNK)), full((D_BRANCH, D_BRANCH)), full((D_BRANCH, D_BRANCH)),
            full((CHUNK, D_BRANCH)),
        ],
        out_specs=pl.BlockSpec((seq, D_BRANCH), lambda b: (b, 0)),
        out_shape=jax.ShapeDtypeStruct((batch * seq, D_BRANCH), F32),
        scratch_shapes=[
            pltpu.VMEM((D_BRANCH, D_BRANCH), F32),
            pltpu.VMEM((CHUNK, D_BRANCH), F32),
            pltpu.VMEM((CHUNK, D_BRANCH), F32),
            pltpu.VMEM((CHUNK, D_BRANCH), F32),
            pltpu.VMEM((HG_JGROUP * CHUNK, D_BRANCH), BF16),
        ],
        compiler_params=_cparams(),
        name="hgrn",
    )(proj, jnp.log(lb), jnp.log1p(-lb), 1.0 - lb, o_gain.reshape(1, D_BRANCH),
      _tri64(), bd, bd.astype(F32), jnp.ones((CHUNK, D_BRANCH), BF16))


def _ssd_kernel(xbc_ref, z_ref, dt_ref, cw_ref, cb_ref, dtb_ref, alog_ref, dx_ref, ng_ref,
                tri_ref, e_ref, u_ref, gm_ref, hm_ref, o_ref, sn_ref):
    seq = xbc_ref.shape[0]
    sn_ref[...] = jnp.zeros_like(sn_ref)
    lidx = lax.broadcasted_iota(jnp.int32, (CHUNK, D_BRANCH), 0)
    sidx = lax.broadcasted_iota(jnp.int32, (CHUNK, D_BRANCH), 1) % CHUNK
    causal = lidx >= sidx
    lane128 = lax.broadcasted_iota(jnp.int32, (1, 2 * SSD_DSTATE), 1)
    a_neg = -jnp.exp(alog_ref[...])

    def chunk(c, carry):
        r0 = pl.multiple_of(c * CHUNK, CHUNK)
        rows = pl.ds(r0, CHUNK)
        cur = xbc_ref[rows, :]
        prev = xbc_ref[pl.ds(pl.multiple_of(jnp.maximum(r0 - 8, 0), 8), 8), :]
        prev = prev * jnp.where(c > 0, 1.0, 0.0)
        ext = jnp.concatenate([prev, cur], axis=0)
        conv = cb_ref[...] + cur * cw_ref[SSD_CONV - 1:SSD_CONV, :]
        for s in range(1, SSD_CONV):
            sh = pltpu.roll(ext, s, axis=0)[8:8 + CHUNK, :]
            conv = conv + sh * cw_ref[SSD_CONV - 1 - s:SSD_CONV - s, :]
        xa = conv * _sigmoid(conv)
        xs = xa[:, 0:SSD_INNER]
        bm = xa[:, SSD_INNER:SSD_INNER + 128]
        cm = xa[:, SSD_INNER + 128:SSD_INNER + 256]

        dt = _softplus(dt_ref[rows, :] + dtb_ref[...])
        dtx = _dot_exact_rhs(dt, e_ref[...])
        adtx = _dot_exact_rhs(dt * a_neg, e_ref[...])
        acum = _dot_exact_lhs(tri_ref[...], adtx)
        seg = _dot_exact_lhs(tri_ref[...], adtx * u_ref[...])
        lmat = jnp.exp(jnp.where(causal, seg, NEG_INF))
        xdt = xs * dtx

        bm16 = bm.astype(BF16)
        y = jnp.zeros((CHUNK, D_BRANCH), F32)
        for grp in range(SSD_GROUPS):
            gsel = (lane128 // SSD_DSTATE) == grp
            gmat = _dot_nt(jnp.where(gsel, cm, 0.0).astype(BF16), bm16)
            for h in range(grp * 2, grp * 2 + 2):
                mh = gmat * lmat[:, h * CHUNK:(h + 1) * CHUNK]
                y = y + _dot(mh.astype(BF16), (xdt * hm_ref[h:h + 1, :]).astype(BF16))

        sn_old = sn_ref[...]
        y = y + _dot(cm.astype(BF16), sn_old.astype(BF16)) * jnp.exp(acum)
        aend = acum[CHUNK - 1:CHUNK, :]
        upd = _dot_tn(bm16, (jnp.exp(aend - acum) * xdt).astype(BF16))
        sn_ref[...] = jnp.exp(aend) * sn_old + upd * gm_ref[...]

        y = y + dx_ref[...] * xs
        zz = z_ref[rows, :]
        y = y * (zz * _sigmoid(zz))
        halves = []
        for grp in range(SSD_GROUPS):
            yh = y[:, grp * 128:(grp + 1) * 128]
            halves.append(yh * lax.rsqrt(jnp.mean(yh * yh, axis=-1, keepdims=True) + EPS))
        o_ref[rows, :] = jnp.concatenate(halves, axis=1) * ng_ref[...]
        return carry

    lax.fori_loop(0, seq // CHUNK, chunk, 0)


def _ssd(proj, conv_w, conv_b, dt_bias, a_log, d_skip, norm_g, batch, seq):
    pad4 = lambda v: jnp.pad(v.reshape(1, SSD_HEADS), ((0, 0), (0, DT_PAD - SSD_HEADS)))
    full = lambda shape: pl.BlockSpec(shape, lambda b: (0,) * len(shape))
    e = np.zeros((DT_PAD, D_BRANCH), np.float32)
    for h in range(SSD_HEADS):
        e[h, h * 64:(h + 1) * 64] = 1.0
    li = np.arange(CHUNK)
    u_t = np.tile((li[:, None] > li[None, :]).astype(np.float32), (1, SSD_HEADS))
    gm = ((np.arange(128) // 64)[:, None] == (np.arange(256) // 128)[None, :]).astype(np.float32)
    return pl.pallas_call(
        _ssd_kernel,
        grid=(batch,),
        in_specs=[
            pl.BlockSpec((seq, SSD_XBC), lambda b: (b, COL_XBC // SSD_XBC)),
            pl.BlockSpec((seq, SSD_INNER), lambda b: (b, COL_Z // SSD_INNER)),
            pl.BlockSpec((seq, DT_PAD), lambda b: (b, COL_DT // DT_PAD)),
            full((SSD_CONV, SSD_XBC)), full((1, SSD_XBC)), full((1, DT_PAD)), full((1, DT_PAD)),
            full((1, D_BRANCH)), full((1, D_BRANCH)),
            full((CHUNK, CHUNK)), full((DT_PAD, D_BRANCH)), full((CHUNK, D_BRANCH)),
            full((128, D_BRANCH)), full((4, D_BRANCH)),
        ],
        out_specs=pl.BlockSpec((seq, D_BRANCH), lambda b: (b, 0)),
        out_shape=jax.ShapeDtypeStruct((batch * seq, D_BRANCH), F32),
        scratch_shapes=[pltpu.VMEM((2 * SSD_DSTATE, D_BRANCH), F32)],
        compiler_params=_cparams(),
        name="ssd",
    )(proj, proj, proj, conv_w.T, conv_b.reshape(1, SSD_XBC), pad4(dt_bias), pad4(a_log),
      jnp.repeat(d_skip, SSD_HEADDIM).reshape(1, D_BRANCH), norm_g.reshape(1, D_BRANCH),
      _tri64(), jnp.asarray(e, BF16), jnp.asarray(u_t), jnp.asarray(gm), _head_masks())


S5_RT = 256


def _gelu_tanh(x):
    return 0.5 * x * (1.0 + jnp.tanh(math.sqrt(2.0 / math.pi) * (x + 0.044715 * (x * x * x))))


def _s5_kernel(ua_ref, ub_ref, kbd_ref, wst_ref, wout_ref, lam_ref, d_ref, wglu_ref, o_ref,
               hloc_s, hprev_s, ya_s, yb_s):
    u_refs = (ua_ref, ub_ref)
    y_refs = (ya_s, yb_s)
    seq = ua_ref.shape[0]
    nchunk = seq // S5_L
    hs = S5_HALF_STATE
    rowmod = lax.broadcasted_iota(jnp.int32, (S5_RT, D_BRANCH), 0) % S5_L

    def tile(i, carry):
        rows = pl.ds(pl.multiple_of(i * S5_RT, S5_RT), S5_RT)
        ut = jnp.concatenate([ua_ref[rows, :], ub_ref[rows, :]], axis=1)
        acc = _dot(ut.astype(BF16), kbd_ref[0])
        for d in range(1, S5_L):
            ud = jnp.where(rowmod >= d, pltpu.roll(ut, d, axis=0), 0.0)
            acc = acc + _dot(ud.astype(BF16), kbd_ref[d])
        ya_s[rows, :] = acc[:, 0:128]
        yb_s[rows, :] = acc[:, 128:256]
        return carry

    lax.fori_loop(0, seq // S5_RT, tile, 0)

    for half in range(2):
        acc = None
        for s in range(S5_L):
            us = u_refs[half][pl.ds(s, nchunk, stride=S5_L), :]
            term = _dot(us.astype(BF16), wst_ref[half, s])
            acc = term if acc is None else acc + term
        hloc_s[half] = acc

    for half in range(2):
        lr = lam_ref[half, 0:1, :]
        li = lam_ref[half, 1:2, :]

        def step(c, carry):
            hr, hi = carry
            hprev_s[half, pl.ds(c, 1), 0:hs] = hr
            hprev_s[half, pl.ds(c, 1), hs:2 * hs] = hi
            loc = hloc_s[half, pl.ds(c, 1), :]
            return (lr * hr - li * hi + loc[:, 0:hs], lr * hi + li * hr + loc[:, hs:2 * hs])

        zero = jnp.zeros((1, hs), F32)
        lax.fori_loop(0, nchunk, step, (zero, zero))

    for half in range(2):
        hp = hprev_s[half].astype(BF16)
        for s in range(S5_L):
            srows = pl.ds(s, nchunk, stride=S5_L)
            y_refs[half][srows, :] = y_refs[half][srows, :] + _dot(hp, wout_ref[half, s])

    def tail(i, carry):
        rows = pl.ds(pl.multiple_of(i * S5_RT, S5_RT), S5_RT)
        ut = jnp.concatenate([ua_ref[rows, :], ub_ref[rows, :]], axis=1)
        yt = jnp.concatenate([ya_s[rows, :], yb_s[rows, :]], axis=1)
        y = _gelu_tanh(yt + d_ref[...] * ut)
        o_ref[rows, :] = y * _sigmoid(_dot(y.astype(BF16), wglu_ref[...]))
        return carry

    lax.fori_loop(0, seq // S5_RT, tail, 0)


def _s5_weights(a_re, a_im, b_re, b_im, c_re, c_im, log_dt):
    hp = lax.Precision.HIGHEST
    step = jnp.exp(log_dt)[:, None]
    mag = jnp.exp(a_re * step)
    lam_re = mag * jnp.cos(a_im * step)
    lam_im = mag * jnp.sin(a_im * step)
    den = a_re * a_re + a_im * a_im
    num_re = lam_re - 1.0
    coef_re = (num_re * a_re + lam_im * a_im) / den
    coef_im = (lam_im * a_re - num_re * a_im) / den
    bb_re = coef_re[..., None] * b_re - coef_im[..., None] * b_im
    bb_im = coef_re[..., None] * b_im + coef_im[..., None] * b_re

    pw_re = [jnp.ones_like(lam_re)]
    pw_im = [jnp.zeros_like(lam_im)]
    for _ in range(S5_L):
        pr, pi = pw_re[-1], pw_im[-1]
        pw_re.append(pr * lam_re - pi * lam_im)
        pw_im.append(pr * lam_im + pi * lam_re)
    rev_re = jnp.stack(pw_re[S5_L - 1::-1])
    rev_im = jnp.stack(pw_im[S5_L - 1::-1])
    pw_re = jnp.stack(pw_re)
    pw_im = jnp.stack(pw_im)

    cp_re = c_re[None] * pw_re[:, :, None, :] - c_im[None] * pw_im[:, :, None, :]
    cp_im = c_re[None] * pw_im[:, :, None, :] + c_im[None] * pw_re[:, :, None, :]
    kd = (jnp.einsum('dgip,gpj->dgij', cp_re[:S5_L], bb_re, precision=hp)
          - jnp.einsum('dgip,gpj->dgij', cp_im[:S5_L], bb_im, precision=hp))
    eye_g = jnp.eye(S5_GROUPS, dtype=F32)
    kbd = jnp.einsum('dgij,gh->dgjhi', kd, eye_g).reshape(S5_L, D_BRANCH, D_BRANCH)

    wb_re = rev_re[..., None] * bb_re[None] - rev_im[..., None] * bb_im[None]
    wb_im = rev_re[..., None] * bb_im[None] + rev_im[..., None] * bb_re[None]
    eye_h = jnp.eye(S5_HALF_G, dtype=F32)

    def halves(t):
        return jnp.moveaxis(t.reshape((S5_L, 2, S5_HALF_G) + t.shape[2:]), 1, 0)

    wst = jnp.stack([jnp.einsum('hsgpj,ga->hsgjap', halves(wb_re), eye_h),
                     jnp.einsum('hsgpj,ga->hsgjap', halves(wb_im), eye_h)], axis=4)
    wst = wst.reshape(2, S5_L, 128, 2 * S5_HALF_STATE)
    wout = jnp.stack([jnp.einsum('hsgip,ga->hsapgi', halves(cp_re[1:S5_L + 1]), eye_h),
                      jnp.einsum('hsgip,ga->hsapgi', halves(-cp_im[1:S5_L + 1]), eye_h)], axis=2)
    wout = wout.reshape(2, S5_L, 2 * S5_HALF_STATE, 128)
    lam_l = jnp.stack([pw_re[S5_L].reshape(2, S5_HALF_STATE),
                       pw_im[S5_L].reshape(2, S5_HALF_STATE)], axis=1)
    return kbd.astype(BF16), wst.astype(BF16), wout.astype(BF16), lam_l


def _s5(proj, a_re, a_im, b_re, b_im, c_re, c_im, d_skip, log_dt, w_glu, batch, seq):
    kbd, wst, wout, lam_l = _s5_weights(a_re, a_im, b_re, b_im, c_re, c_im, log_dt)
    nchunk = seq // S5_L
    full = lambda shape: pl.BlockSpec(shape, lambda b: (0,) * len(shape))
    return pl.pallas_call(
        _s5_kernel,
        grid=(batch,),
        in_specs=[
            pl.BlockSpec((seq, 128), lambda b: (b, COL_U // 128)),
            pl.BlockSpec((seq, 128), lambda b: (b, COL_U // 128 + 1)),
            full((S5_L, D_BRANCH, D_BRANCH)),
            full((2, S5_L, 128, 2 * S5_HALF_STATE)),
            full((2, S5_L, 2 * S5_HALF_STATE, 128)),
            full((2, 2, S5_HALF_STATE)),
            full((1, D_BRANCH)),
            full((D_BRANCH, D_BRANCH)),
        ],
        out_specs=pl.BlockSpec((seq, D_BRANCH), lambda b: (b, 0)),
        out_shape=jax.ShapeDtypeStruct((batch * seq, D_BRANCH), F32),
        scratch_shapes=[
            pltpu.VMEM((2, nchunk, 2 * S5_HALF_STATE), F32),
            pltpu.VMEM((2, nchunk, 2 * S5_HALF_STATE), F32),
            pltpu.VMEM((seq, 128), F32),
            pltpu.VMEM((seq, 128), F32),
        ],
        compiler_params=_cparams(),
        name="s5",
    )(proj, proj, kbd, wst, wout, lam_l, d_skip.reshape(1, D_BRANCH), w_glu.astype(BF16))


def _attn_kernel(q_ref, k_ref, v_ref, qg_ref, kg_ref, bias_ref, onesbd_ref, hm_ref, o_ref,
                 kp_s, vp_s):
    seq = q_ref.shape[0]
    kp_s[0:ATT_PAD, :] = jnp.zeros((ATT_PAD, D_BRANCH), BF16)
    vp_s[0:ATT_PAD, :] = jnp.zeros((ATT_PAD, D_BRANCH), BF16)
    rt = 256

    def prep(i, carry):
        rows = pl.ds(pl.multiple_of(i * rt, rt), rt)
        k = k_ref[rows, :]
        ms = _dot_exact_rhs(k * k, onesbd_ref[...]) * (1.0 / ATT_HEADDIM)
        kn = k * lax.rsqrt(ms + EPS) * kg_ref[...]
        prow = pl.ds(pl.multiple_of(ATT_PAD + i * rt, 64), rt)
        kp_s[prow, :] = kn.astype(BF16)
        vp_s[prow, :] = v_ref[rows, :].astype(BF16)
        return carry

    lax.fori_loop(0, seq // rt, prep, 0)

    pidx = lax.broadcasted_iota(jnp.int32, (ATT_HEADS * CHUNK, ATT_BAND), 1)
    scale = ATT_HEADDIM ** -0.5

    def chunk(c, carry):
        r0 = pl.multiple_of(c * CHUNK, CHUNK)
        q = q_ref[pl.ds(r0, CHUNK), :]
        ms = _dot_exact_rhs(q * q, onesbd_ref[...]) * (1.0 / ATT_HEADDIM)
        qn = q * lax.rsqrt(ms + EPS) * (qg_ref[...] * scale)
        qs = jnp.concatenate([qn * hm_ref[h:h + 1, :] for h in range(ATT_HEADS)], axis=0).astype(BF16)
        kb = kp_s[pl.ds(r0, ATT_BAND), :]
        vb = vp_s[pl.ds(r0, ATT_BAND), :]
        s = _dot_nt(qs, kb)
        first_valid = jnp.maximum(ATT_LEFT_CHUNKS + 1 - c, 1) * CHUNK
        s = jnp.where(pidx >= first_valid, s + bias_ref[...], NEG_INF)
        m = jnp.max(s, axis=-1, keepdims=True)
        e = jnp.exp(s - m)
        l = jnp.sum(e, axis=-1, keepdims=True)
        pv = _dot(e.astype(BF16), vb) / l
        out = pv[0:CHUNK, :] * hm_ref[0:1, :]
        for h in range(1, ATT_HEADS):
            out = out + pv[h * CHUNK:(h + 1) * CHUNK, :] * hm_ref[h:h + 1, :]
        o_ref[pl.ds(r0, CHUNK), :] = out
        return carry

    lax.fori_loop(0, seq // CHUNK, chunk, 0)


def _attn(proj, q_gain, k_gain, rel_bias, batch, seq):
    qpos = np.arange(CHUNK)[:, None]
    kpos = np.arange(ATT_BAND)[None, :] - CHUNK
    rel = np.clip(qpos + ATT_LEFT_CHUNKS * CHUNK - kpos, -ATT_MAX_REL, ATT_MAX_REL) + ATT_MAX_REL
    bias = rel_bias[:, rel].reshape(ATT_HEADS * CHUNK, ATT_BAND)
    full = lambda shape: pl.BlockSpec(shape, lambda b: (0,) * len(shape))
    colblk = lambda j: pl.BlockSpec((seq, D_BRANCH), lambda b: (b, COL_QKV // D_BRANCH + j))
    return pl.pallas_call(
        _attn_kernel,
        grid=(batch,),
        in_specs=[
            colblk(0), colblk(1), colblk(2),
            full((1, D_BRANCH)), full((1, D_BRANCH)),
            full((ATT_HEADS * CHUNK, ATT_BAND)),
            full((D_BRANCH, D_BRANCH)), full((4, D_BRANCH)),
        ],
        out_specs=pl.BlockSpec((seq, D_BRANCH), lambda b: (b, 0)),
        out_shape=jax.ShapeDtypeStruct((batch * seq, D_BRANCH), F32),
        scratch_shapes=[
            pltpu.VMEM((ATT_PAD + seq, D_BRANCH), BF16),
            pltpu.VMEM((ATT_PAD + seq, D_BRANCH), BF16),
        ],
        compiler_params=_cparams(),
        name="attn",
    )(proj, proj, proj, jnp.tile(q_gain, ATT_HEADS).reshape(1, D_BRANCH),
      jnp.tile(k_gain, ATT_HEADS).reshape(1, D_BRANCH), bias, _ones_bd(64), _head_masks())


def kernel(x, p, norm_mix, w_in, hg_lb_logits, hg_o_norm, ssd_conv_w, ssd_conv_b, ssd_dt_bias, ssd_A_log, ssd_D, ssd_norm, s5_A_re, s5_A_im, s5_B_re, s5_B_im, s5_C_re, s5_C_im, s5_D, s5_log_dt, s5_w_glu, att_q_norm, att_k_norm, att_rel_bias, w_branch, w_out, norm_ffn, w_ff1, w_ff2, w_ple, norm_ple, w_ple_gate):
    batch, seq, _ = x.shape
    depth = w_in.shape[0]
    t = batch * seq
    lb_all = jnp.cumsum(jax.nn.softmax(hg_lb_logits.astype(F32), axis=0), axis=0)
    lb_all = lb_all - lb_all[0:1]

    x2 = x.reshape(t, D_MODEL)
    for i in range(depth):
        wi = w_in[i]
        w_mix = jnp.concatenate(
            [wi[:, 0:1024], wi[:, 1280:1792], wi[:, 1024:1280], wi[:, 1796:2052], wi[:, 2052:2820],
             wi[:, 1792:1796], jnp.zeros((D_MODEL, DT_PAD - SSD_HEADS), F32)], axis=1).astype(BF16)
        w_gate = wi[:, 2820:].astype(BF16)
        g_mix = norm_mix[i].reshape(1, D_MODEL)

        proj = _inproj(x2, g_mix, w_mix)
        y_a = _hgrn(proj, lb_all[i], hg_o_norm[i], batch, seq)
        y_b = _ssd(proj, ssd_conv_w[i], ssd_conv_b[i], ssd_dt_bias[i], ssd_A_log[i], ssd_D[i],
                   ssd_norm[i], batch, seq)
        y_c = _s5(proj, s5_A_re[i], s5_A_im[i], s5_B_re[i], s5_B_im[i], s5_C_re[i], s5_C_im[i],
                  s5_D[i], s5_log_dt[i], s5_w_glu[i], batch, seq)
        y_d = _attn(proj, att_q_norm[i], att_k_norm[i], att_rel_bias[i], batch, seq)

        x2 = _merge(x2, y_a, y_b, y_c, y_d, g_mix, w_gate, w_branch[i].astype(BF16),
                    w_out[i].astype(BF16))
        x2 = _ffn(x2, norm_ffn[i].reshape(1, D_MODEL), w_ff1[i].astype(BF16), w_ff2[i].astype(BF16))
        x2 = _ple(x2, p[i].reshape(t, D_PLE), norm_ple[i].reshape(1, D_MODEL),
                  w_ple_gate[i].astype(BF16), w_ple[i].astype(BF16))
    return x2.reshape(batch, seq, D_MODEL)
```

```python
import functools
import math

import jax
import jax.numpy as jnp
import numpy as np
from jax import lax
from jax.experimental import pallas as pl
from jax.experimental.pallas import tpu as pltpu

F32 = jnp.float32
BF16 = jnp.bfloat16

D_MODEL = 1024
CHUNK = 64
D_PLE = 256
N_BRANCH = 4
D_BRANCH = 256
D_FF = 4096
EPS = 1e-6
NEG_INF = -1e30

HG_HEADS = 4
HG_DK = 64
SSD_HEADS = 4
SSD_HEADDIM = 64
SSD_GROUPS = 2
SSD_DSTATE = 64
SSD_CONV = 4
SSD_INNER = 256
SSD_XBC = 512
S5_GROUP_CH = 16
S5_GROUPS = 16
S5_STATE = 64
ATT_HEADS = 4
ATT_HEADDIM = 64
ATT_LEFT_CHUNKS = 8
ATT_MAX_REL = 128

COL_HG = 0
COL_XBC = 1024
COL_Z = 1536
COL_U = 1792
COL_QKV = 2048
COL_DT = 2816
N_MIX = 2944
DT_PAD = 128

W_HG, W_Z, W_XBC, W_DT, W_U, W_QKV, W_GATE = 0, 1024, 1280, 1792, 1796, 2052, 2820
N_IN = 6916

S5_L = 8
S5_HALF_G = 8
S5_HALF_STATE = S5_HALF_G * S5_STATE

ATT_BAND = (ATT_LEFT_CHUNKS + 2) * CHUNK
ATT_PAD = (ATT_LEFT_CHUNKS + 1) * CHUNK

VMEM_LIMIT = 56 * 1024 * 1024


def _cparams(n_axes=1):
    return pltpu.CompilerParams(
        dimension_semantics=("arbitrary",) * n_axes, vmem_limit_bytes=VMEM_LIMIT)


def _dot(a, b):
    return jnp.dot(a, b, preferred_element_type=F32)


def _dot_nt(a, b):
    return lax.dot_general(a, b, (((1,), (1,)), ((), ())), preferred_element_type=F32)


def _dot_tn(a, b):
    return lax.dot_general(a, b, (((0,), (0,)), ((), ())), preferred_element_type=F32)


def _split3(a):
    a1 = a.astype(BF16)
    r1 = a - a1.astype(F32)
    a2 = r1.astype(BF16)
    r2 = r1 - a2.astype(F32)
    return a1, a2, r2.astype(BF16)


def _dot_exact_rhs(a, m):
    a1, a2, a3 = _split3(a)
    return _dot(a1, m) + _dot(a2, m) + _dot(a3, m)


def _dot_exact_lhs(m, a):
    a1, a2, a3 = _split3(a)
    return _dot(m, a1) + _dot(m, a2) + _dot(m, a3)


def _dot_tn_exact_rhs(a, m):
    a1, a2, a3 = _split3(a)
    return _dot_tn(a1, m) + _dot_tn(a2, m) + _dot_tn(a3, m)


def _sigmoid(x):
    return 1.0 / (1.0 + jnp.exp(-x))


def _softplus(x):
    return jnp.maximum(x, 0.0) + jnp.log1p(jnp.exp(-jnp.abs(x)))


def _rms_rows(x, g):
    return x * lax.rsqrt(jnp.mean(x * x, axis=-1, keepdims=True) + EPS) * g


def _repack_kernel(w_ref, mix_ref, gate_ref):
    def piece(lo, hi):
        return w_ref[:, lo:hi].astype(BF16)

    mix_ref[:, COL_HG:COL_HG + 1024] = piece(W_HG, W_HG + 1024)
    mix_ref[:, COL_XBC:COL_XBC + SSD_XBC] = piece(W_XBC, W_XBC + SSD_XBC)
    mix_ref[:, COL_Z:COL_Z + SSD_INNER] = piece(W_Z, W_Z + SSD_INNER)
    mix_ref[:, COL_U:COL_U + D_BRANCH] = piece(W_U, W_U + D_BRANCH)
    mix_ref[:, COL_QKV:COL_QKV + 3 * D_BRANCH] = piece(W_QKV, W_QKV + 3 * D_BRANCH)
    lane = lax.broadcasted_iota(jnp.int32, (w_ref.shape[0], DT_PAD), 1)
    dt = jnp.where(lane < SSD_HEADS, w_ref[:, W_DT:W_DT + DT_PAD], 0.0)
    mix_ref[:, COL_DT:COL_DT + DT_PAD] = dt.astype(BF16)
    gate_ref[...] = piece(W_GATE, N_IN)


def _repack(w_in, tk=128):
    depth = w_in.shape[0]
    return pl.pallas_call(
        _repack_kernel,
        grid=(depth, D_MODEL // tk),
        in_specs=[pl.BlockSpec((None, tk, N_IN), lambda l, r: (l, r, 0))],
        out_specs=[pl.BlockSpec((None, tk, N_MIX), lambda l, r: (l, r, 0)),
                   pl.BlockSpec((None, tk, N_BRANCH * D_MODEL), lambda l, r: (l, r, 0))],
        out_shape=[jax.ShapeDtypeStruct((depth, D_MODEL, N_MIX), BF16),
                   jax.ShapeDtypeStruct((depth, D_MODEL, N_BRANCH * D_MODEL), BF16)],
        compiler_params=_cparams(2),
        name="repack",
    )(w_in)


def _inproj_kernel(x_ref, g_ref, w_ref, o_ref):
    h = _rms_rows(x_ref[...], g_ref[...])
    o_ref[...] = _dot(h.astype(BF16), w_ref[...])


def _inproj(x2, g, w_mix, layer, tm=512):
    t = x2.shape[0]
    return pl.pallas_call(
        _inproj_kernel,
        grid=(t // tm,),
        in_specs=[
            pl.BlockSpec((tm, D_MODEL), lambda i: (i, 0)),
            pl.BlockSpec((1, D_MODEL), lambda i: (0, 0)),
            pl.BlockSpec((None, D_MODEL, N_MIX), lambda i: (layer, 0, 0)),
        ],
        out_specs=pl.BlockSpec((tm, N_MIX), lambda i: (i, 0)),
        out_shape=jax.ShapeDtypeStruct((t, N_MIX), F32),
        compiler_params=_cparams(),
        name="inproj",
    )(x2, g, w_mix)


def _merge_kernel(x_ref, ya_ref, yb_ref, yc_ref, yd_ref, g_ref, wg_ref, wb_ref, wo_ref, o_ref):
    x = x_ref[...]
    h = _rms_rows(x, g_ref[...]).astype(BF16)
    merged = None
    for m, y_ref in enumerate((ya_ref, yb_ref, yc_ref, yd_ref)):
        gate = _sigmoid(_dot(h, wg_ref[:, m * D_MODEL:(m + 1) * D_MODEL]))
        term = gate * _dot(y_ref[...].astype(BF16), wb_ref[m])
        merged = term if merged is None else merged + term
    o_ref[...] = x + _dot(merged.astype(BF16), wo_ref[...])


def _merge(x2, ya, yb, yc, yd, g, w_gate, w_branch, w_out, layer, tm=256):
    t = x2.shape[0]
    row = lambda i: (i, 0)
    return pl.pallas_call(
        _merge_kernel,
        grid=(t // tm,),
        in_specs=[
            pl.BlockSpec((tm, D_MODEL), row),
            pl.BlockSpec((tm, D_BRANCH), row),
            pl.BlockSpec((tm, D_BRANCH), row),
            pl.BlockSpec((tm, D_BRANCH), row),
            pl.BlockSpec((tm, D_BRANCH), row),
            pl.BlockSpec((1, D_MODEL), lambda i: (0, 0)),
            pl.BlockSpec((None, D_MODEL, N_BRANCH * D_MODEL), lambda i: (layer, 0, 0)),
            pl.BlockSpec((N_BRANCH, D_BRANCH, D_MODEL), lambda i: (0, 0, 0)),
            pl.BlockSpec((D_MODEL, D_MODEL), lambda i: (0, 0)),
        ],
        out_specs=pl.BlockSpec((tm, D_MODEL), row),
        out_shape=jax.ShapeDtypeStruct((t, D_MODEL), F32),
        compiler_params=_cparams(),
        name="merge",
    )(x2, ya, yb, yc, yd, g, w_gate, w_branch, w_out)


def _ffn_kernel(x_ref, g_ref, w1_ref, w2_ref, o_ref):
    x = x_ref[...]
    h = _rms_rows(x, g_ref[...]).astype(BF16)
    acc = x
    tk = 1024
    for j in range(D_FF // tk):
        a = jnp.maximum(_dot(h, w1_ref[:, j * tk:(j + 1) * tk]), 0.0)
        acc = acc + _dot((a * a).astype(BF16), w2_ref[j * tk:(j + 1) * tk, :])
    o_ref[...] = acc


def _ffn(x2, g, w1, w2, tm=256):
    t = x2.shape[0]
    return pl.pallas_call(
        _ffn_kernel,
        grid=(t // tm,),
        in_specs=[
            pl.BlockSpec((tm, D_MODEL), lambda i: (i, 0)),
            pl.BlockSpec((1, D_MODEL), lambda i: (0, 0)),
            pl.BlockSpec((D_MODEL, D_FF), lambda i: (0, 0)),
            pl.BlockSpec((D_FF, D_MODEL), lambda i: (0, 0)),
        ],
        out_specs=pl.BlockSpec((tm, D_MODEL), lambda i: (i, 0)),
        out_shape=jax.ShapeDtypeStruct((t, D_MODEL), F32),
        compiler_params=_cparams(),
        name="ffn",
    )(x2, g, w1, w2)


def _ple_kernel(x_ref, p_ref, g_ref, wg_ref, wp_ref, o_ref):
    x = x_ref[...]
    h = _rms_rows(x, g_ref[...]).astype(BF16)
    gate = _sigmoid(_dot(h, wg_ref[...]))
    o_ref[...] = x + _dot(p_ref[...].astype(BF16), wp_ref[...]) * gate


def _ple(x2, p3, g, w_gate, w_ple, layer, tm=512):
    t = x2.shape[0]
    return pl.pallas_call(
        _ple_kernel,
        grid=(t // tm,),
        in_specs=[
            pl.BlockSpec((tm, D_MODEL), lambda i: (i, 0)),
            pl.BlockSpec((None, tm, D_PLE), lambda i: (layer, i, 0)),
            pl.BlockSpec((1, D_MODEL), lambda i: (0, 0)),
            pl.BlockSpec((D_MODEL, D_MODEL), lambda i: (0, 0)),
            pl.BlockSpec((D_PLE, D_MODEL), lambda i: (0, 0)),
        ],
        out_specs=pl.BlockSpec((tm, D_MODEL), lambda i: (i, 0)),
        out_shape=jax.ShapeDtypeStruct((t, D_MODEL), F32),
        compiler_params=_cparams(),
        name="ple",
    )(x2, p3, g, w_gate, w_ple)


def _tri64():
    i = np.arange(CHUNK)
    return jnp.asarray(i[:, None] >= i[None, :], BF16)


def _ones_bd(block, n=D_BRANCH):
    i = np.arange(n) // block
    return jnp.asarray(i[:, None] == i[None, :], BF16)


def _head_masks():
    lane = np.arange(D_BRANCH) // 64
    return jnp.asarray(lane[None, :] == np.arange(4)[:, None], F32)


HG_SUB = 16
HG_TILE = 256
HG_SUB_STACK = 8 * HG_SUB + 4 * HG_SUB


def _hgrn_kernel(p_ref, loglb_ref, log1mlb_ref, omlb_ref, gain_ref, tri_ref, onesbd_ref,
                 bdmask_ref, o_ref, st_ref, bl_s, kk_s, w_s):
    seq = p_ref.shape[0]
    half = HG_SUB // 2

    def pre(i, carry):
        rows = pl.ds(pl.multiple_of(i * HG_TILE, HG_TILE), HG_TILE)
        z = p_ref[rows, 256:512]
        y = log1mlb_ref[...] + (jnp.minimum(z, 0.0) - jnp.log1p(jnp.exp(-jnp.abs(z))))
        a = loglb_ref[...]
        lf = jnp.maximum(a, y) + jnp.log1p(jnp.exp(-jnp.abs(a - y)))
        kk_s[rows, :] = omlb_ref[...] * _sigmoid(-z)
        bl_s[rows, :] = _dot_exact_lhs(tri_ref[...], lf)
        return carry

    lax.fori_loop(0, seq // HG_TILE, pre, 0)

    st_ref[...] = jnp.zeros_like(st_ref)
    row16 = lax.broadcasted_iota(jnp.int32, (HG_SUB, D_BRANCH), 0)
    row8 = lax.broadcasted_iota(jnp.int32, (half, D_BRANCH), 0)

    def chunk(c, carry):
        r0 = c * CHUNK
        subs = []
        for s in range(CHUNK // HG_SUB):
            rs = pl.multiple_of(r0 + s * HG_SUB, HG_SUB)
            rows = pl.ds(rs, HG_SUB)
            q = p_ref[rows, 0:256]
            bl = bl_s[rows, :]
            base = s * HG_SUB_STACK
            for j in range(half):
                d = jnp.where(row16 >= j, bl - bl_s[pl.ds(rs + j, 1), :], NEG_INF)
                w = q * jnp.exp(d) * kk_s[pl.ds(rs + j, 1), :]
                w_s[base + j * HG_SUB:base + (j + 1) * HG_SUB, :] = w.astype(BF16)
            q_hi = q[half:, :]
            bl_hi = bl[half:, :]
            for jp in range(half // 2):
                pieces = []
                for j in (half + 2 * jp, half + 2 * jp + 1):
                    d = jnp.where(row8 >= j - half, bl_hi - bl_s[pl.ds(rs + j, 1), :], NEG_INF)
                    pieces.append(q_hi * jnp.exp(d) * kk_s[pl.ds(rs + j, 1), :])
                off = base + half * HG_SUB + jp * HG_SUB
                w_s[off:off + HG_SUB, :] = jnp.concatenate(pieces, axis=0).astype(BF16)
            subs.append((rs, rows, q, bl))

        r = _dot(w_s[...], onesbd_ref[...])

        st = st_ref[...]
        for s, (rs, rows, q, bl) in enumerate(subs):
            base = s * HG_SUB_STACK
            v = p_ref[rows, 512:768]
            o_lo = jnp.zeros((half, D_BRANCH), F32)
            o_hi = jnp.zeros((half, D_BRANCH), F32)
            for j in range(HG_SUB):
                vj = p_ref[pl.ds(rs + j, 1), 512:768]
                if j < half:
                    o_lo = o_lo + r[base + j * HG_SUB:base + j * HG_SUB + half, :] * vj
                    o_hi = o_hi + r[base + j * HG_SUB + half:base + (j + 1) * HG_SUB, :] * vj
                else:
                    off = base + half * HG_SUB + (j - half) * half
                    o_hi = o_hi + r[off:off + half, :] * vj
            o = jnp.concatenate([o_lo, o_hi], axis=0)
            o = o + _dot_nt((q * jnp.exp(bl)).astype(BF16), st.astype(BF16))
            bend = bl[HG_SUB - 1:HG_SUB, :]
            ka = (kk_s[rows, :] * jnp.exp(bend - bl)).astype(BF16)
            st = st * jnp.exp(bend) + _dot_tn(v.astype(BF16), ka) * bdmask_ref[...]
            o_ref[rows, :] = o
        st_ref[...] = st
        return carry

    lax.fori_loop(0, seq // CHUNK, chunk, 0)

    def post(i, carry):
        rows = pl.ds(pl.multiple_of(i * HG_TILE, HG_TILE), HG_TILE)
        o = o_ref[rows, :]
        g = p_ref[rows, 768:1024]
        ms = _dot_exact_rhs(o * o, onesbd_ref[...]) * (1.0 / HG_DK)
        o_ref[rows, :] = o * lax.rsqrt(ms + EPS) * gain_ref[...] * (g * _sigmoid(g))
        return carry

    lax.fori_loop(0, seq // HG_TILE, post, 0)


def _hgrn(proj, lb, o_gain, batch, seq):
    lb = lb.reshape(1, D_BRANCH)
    vec = pl.BlockSpec((1, D_BRANCH), lambda b: (0, 0))
    full = lambda shape: pl.BlockSpec(shape, lambda b: (0,) * len(shape))
    bd = _ones_bd(64)
    i = np.arange(HG_TILE)
    tri = jnp.asarray((i[:, None] // HG_SUB == i[None, :] // HG_SUB) & (i[:, None] >= i[None, :]), BF16)
    return pl.pallas_call(
        _hgrn_kernel,
        grid=(batch,),
        in_specs=[
            pl.BlockSpec((seq, 1024), lambda b: (b, COL_HG // 1024)),
            vec, vec, vec, vec,
            full((HG_TILE, HG_TILE)), full((D_BRANCH, D_BRANCH)), full((D_BRANCH, D_BRANCH)),
        ],
        out_specs=pl.BlockSpec((seq, D_BRANCH), lambda b: (b, 0)),
        out_shape=jax.ShapeDtypeStruct((batch * seq, D_BRANCH), F32),
        scratch_shapes=[
            pltpu.VMEM((D_BRANCH, D_BRANCH), F32),
            pltpu.VMEM((seq, D_BRANCH), F32),
            pltpu.VMEM((seq, D_BRANCH), F32),
            pltpu.VMEM((CHUNK // HG_SUB * HG_SUB_STACK, D_BRANCH), BF16),
        ],
        compiler_params=_cparams(),
        name="hgrn",
    )(proj, jnp.log(lb), jnp.log1p(-lb), 1.0 - lb, o_gain.reshape(1, D_BRANCH),
      tri, bd, bd.astype(F32))


def _ssd_kernel(xbc_ref, z_ref, dt_ref, cw_ref, cb_ref, dtb_ref, alog_ref, dx_ref, ng_ref,
                tri_ref, e_ref, u_ref, gm_ref, hm_ref, o_ref, sn_ref):
    seq = xbc_ref.shape[0]
    sn_ref[...] = jnp.zeros_like(sn_ref)
    lidx = lax.broadcasted_iota(jnp.int32, (CHUNK, D_BRANCH), 0)
    sidx = lax.broadcasted_iota(jnp.int32, (CHUNK, D_BRANCH), 1) % CHUNK
    causal = lidx >= sidx
    lane128 = lax.broadcasted_iota(jnp.int32, (1, 2 * SSD_DSTATE), 1)
    a_neg = -jnp.exp(alog_ref[...])

    def chunk(c, carry):
        r0 = pl.multiple_of(c * CHUNK, CHUNK)
        rows = pl.ds(r0, CHUNK)
        cur = xbc_ref[rows, :]
        prev = xbc_ref[pl.ds(pl.multiple_of(jnp.maximum(r0 - 8, 0), 8), 8), :]
        prev = prev * jnp.where(c > 0, 1.0, 0.0)
        ext = jnp.concatenate([prev, cur], axis=0)
        conv = cb_ref[...] + cur * cw_ref[SSD_CONV - 1:SSD_CONV, :]
        for s in range(1, SSD_CONV):
            sh = pltpu.roll(ext, s, axis=0)[8:8 + CHUNK, :]
            conv = conv + sh * cw_ref[SSD_CONV - 1 - s:SSD_CONV - s, :]
        xa = conv * _sigmoid(conv)
        xs = xa[:, 0:SSD_INNER]
        bm = xa[:, SSD_INNER:SSD_INNER + 128]
        cm = xa[:, SSD_INNER + 128:SSD_INNER + 256]

        dt = _softplus(dt_ref[rows, :] + dtb_ref[...])
        dtx = _dot_exact_rhs(dt, e_ref[...])
        adtx = _dot_exact_rhs(dt * a_neg, e_ref[...])
        acum = _dot_exact_lhs(tri_ref[...], adtx)
        seg = _dot_exact_lhs(tri_ref[...], adtx * u_ref[...])
        lmat = jnp.exp(jnp.where(causal, seg, NEG_INF))
        xdt = xs * dtx

        bm16 = bm.astype(BF16)
        y = jnp.zeros((CHUNK, D_BRANCH), F32)
        for grp in range(SSD_GROUPS):
            gsel = (lane128 // SSD_DSTATE) == grp
            gmat = _dot_nt(jnp.where(gsel, cm, 0.0).astype(BF16), bm16)
            for h in range(grp * 2, grp * 2 + 2):
                mh = gmat * lmat[:, h * CHUNK:(h + 1) * CHUNK]
                y = y + _dot(mh.astype(BF16), (xdt * hm_ref[h:h + 1, :]).astype(BF16))

        sn_old = sn_ref[...]
        y = y + _dot(cm.astype(BF16), sn_old.astype(BF16)) * jnp.exp(acum)
        aend = acum[CHUNK - 1:CHUNK, :]
        upd = _dot_tn(bm16, (jnp.exp(aend - acum) * xdt).astype(BF16))
        sn_ref[...] = jnp.exp(aend) * sn_old + upd * gm_ref[...]

        y = y + dx_ref[...] * xs
        zz = z_ref[rows, :]
        y = y * (zz * _sigmoid(zz))
        halves = []
        for grp in range(SSD_GROUPS):
            yh = y[:, grp * 128:(grp + 1) * 128]
            halves.append(yh * lax.rsqrt(jnp.mean(yh * yh, axis=-1, keepdims=True) + EPS))
        o_ref[rows, :] = jnp.concatenate(halves, axis=1) * ng_ref[...]
        return carry

    lax.fori_loop(0, seq // CHUNK, chunk, 0, unroll=2)


def _ssd(proj, conv_w, conv_b, dt_bias, a_log, d_skip, norm_g, batch, seq):
    pad4 = lambda v: jnp.pad(v.reshape(1, SSD_HEADS), ((0, 0), (0, DT_PAD - SSD_HEADS)))
    full = lambda shape: pl.BlockSpec(shape, lambda b: (0,) * len(shape))
    e = np.zeros((DT_PAD, D_BRANCH), np.float32)
    for h in range(SSD_HEADS):
        e[h, h * 64:(h + 1) * 64] = 1.0
    li = np.arange(CHUNK)
    u_t = np.tile((li[:, None] > li[None, :]).astype(np.float32), (1, SSD_HEADS))
    gm = ((np.arange(128) // 64)[:, None] == (np.arange(256) // 128)[None, :]).astype(np.float32)
    return pl.pallas_call(
        _ssd_kernel,
        grid=(batch,),
        in_specs=[
            pl.BlockSpec((seq, SSD_XBC), lambda b: (b, COL_XBC // SSD_XBC)),
            pl.BlockSpec((seq, SSD_INNER), lambda b: (b, COL_Z // SSD_INNER)),
            pl.BlockSpec((seq, DT_PAD), lambda b: (b, COL_DT // DT_PAD)),
            full((SSD_CONV, SSD_XBC)), full((1, SSD_XBC)), full((1, DT_PAD)), full((1, DT_PAD)),
            full((1, D_BRANCH)), full((1, D_BRANCH)),
            full((CHUNK, CHUNK)), full((DT_PAD, D_BRANCH)), full((CHUNK, D_BRANCH)),
            full((128, D_BRANCH)), full((4, D_BRANCH)),
        ],
        out_specs=pl.BlockSpec((seq, D_BRANCH), lambda b: (b, 0)),
        out_shape=jax.ShapeDtypeStruct((batch * seq, D_BRANCH), F32),
        scratch_shapes=[pltpu.VMEM((2 * SSD_DSTATE, D_BRANCH), F32)],
        compiler_params=_cparams(),
        name="ssd",
    )(proj, proj, proj, conv_w.T, conv_b.reshape(1, SSD_XBC), pad4(dt_bias), pad4(a_log),
      jnp.repeat(d_skip, SSD_HEADDIM).reshape(1, D_BRANCH), norm_g.reshape(1, D_BRANCH),
      _tri64(), jnp.asarray(e, BF16), jnp.asarray(u_t), jnp.asarray(gm), _head_masks())


S5_RT = 256


def _gelu_tanh(x):
    return 0.5 * x * (1.0 + jnp.tanh(math.sqrt(2.0 / math.pi) * (x + 0.044715 * (x * x * x))))


def _s5_kernel(ua_ref, ub_ref, kbd_ref, wst_ref, wout_ref, lam_ref, d_ref, wglu_ref, o_ref,
               hloc_s, hprev_s, ya_s, yb_s):
    u_refs = (ua_ref, ub_ref)
    y_refs = (ya_s, yb_s)
    seq = ua_ref.shape[0]
    nchunk = seq // S5_L
    hs = S5_HALF_STATE
    rowmod = lax.broadcasted_iota(jnp.int32, (S5_RT, D_BRANCH), 0) % S5_L

    def tile(i, carry):
        rows = pl.ds(pl.multiple_of(i * S5_RT, S5_RT), S5_RT)
        ut = jnp.concatenate([ua_ref[rows, :], ub_ref[rows, :]], axis=1)
        acc = _dot(ut.astype(BF16), kbd_ref[0])
        for d in range(1, S5_L):
            ud = jnp.where(rowmod >= d, pltpu.roll(ut, d, axis=0), 0.0)
            acc = acc + _dot(ud.astype(BF16), kbd_ref[d])
        ya_s[rows, :] = acc[:, 0:128]
        yb_s[rows, :] = acc[:, 128:256]
        return carry

    lax.fori_loop(0, seq // S5_RT, tile, 0)

    for half in range(2):
        acc = None
        for s in range(S5_L):
            us = u_refs[half][pl.ds(s, nchunk, stride=S5_L), :]
            term = _dot(us.astype(BF16), wst_ref[half, s])
            acc = term if acc is None else acc + term
        hloc_s[half] = acc

    for half in range(2):
        lr = lam_ref[half, 0:1, :]
        li = lam_ref[half, 1:2, :]

        def step(c, carry):
            hr, hi = carry
            hprev_s[half, pl.ds(c, 1), 0:hs] = hr
            hprev_s[half, pl.ds(c, 1), hs:2 * hs] = hi
            loc = hloc_s[half, pl.ds(c, 1), :]
            return (lr * hr - li * hi + loc[:, 0:hs], lr * hi + li * hr + loc[:, hs:2 * hs])

        zero = jnp.zeros((1, hs), F32)
        lax.fori_loop(0, nchunk, step, (zero, zero))

    for half in range(2):
        hp = hprev_s[half].astype(BF16)
        for s in range(S5_L):
            srows = pl.ds(s, nchunk, stride=S5_L)
            y_refs[half][srows, :] = y_refs[half][srows, :] + _dot_nt(hp, wout_ref[half, s])

    def tail(i, carry):
        rows = pl.ds(pl.multiple_of(i * S5_RT, S5_RT), S5_RT)
        ut = jnp.concatenate([ua_ref[rows, :], ub_ref[rows, :]], axis=1)
        yt = jnp.concatenate([ya_s[rows, :], yb_s[rows, :]], axis=1)
        y = _gelu_tanh(yt + d_ref[...] * ut)
        o_ref[rows, :] = y * _sigmoid(_dot(y.astype(BF16), wglu_ref[...]))
        return carry

    lax.fori_loop(0, seq // S5_RT, tail, 0)


def _s5_weights(a_re, a_im, b_re, b_im, c_re, c_im, log_dt):
    hp = lax.Precision.HIGHEST
    step = jnp.exp(log_dt)[:, None]
    mag = jnp.exp(a_re * step)
    lam_re = mag * jnp.cos(a_im * step)
    lam_im = mag * jnp.sin(a_im * step)
    den = a_re * a_re + a_im * a_im
    num_re = lam_re - 1.0
    coef_re = (num_re * a_re + lam_im * a_im) / den
    coef_im = (lam_im * a_re - num_re * a_im) / den
    bb_re = coef_re[..., None] * b_re - coef_im[..., None] * b_im
    bb_im = coef_re[..., None] * b_im + coef_im[..., None] * b_re

    pw_re = [jnp.ones_like(lam_re)]
    pw_im = [jnp.zeros_like(lam_im)]
    for _ in range(S5_L):
        pr, pi = pw_re[-1], pw_im[-1]
        pw_re.append(pr * lam_re - pi * lam_im)
        pw_im.append(pr * lam_im + pi * lam_re)
    rev_re = jnp.stack(pw_re[S5_L - 1::-1])
    rev_im = jnp.stack(pw_im[S5_L - 1::-1])
    pw_re = jnp.stack(pw_re)
    pw_im = jnp.stack(pw_im)

    cp_re = c_re[None] * pw_re[:, :, None, :] - c_im[None] * pw_im[:, :, None, :]
    cp_im = c_re[None] * pw_im[:, :, None, :] + c_im[None] * pw_re[:, :, None, :]
    kd = (jnp.einsum('dgip,gpj->dgij', cp_re[:S5_L], bb_re, precision=hp)
          - jnp.einsum('dgip,gpj->dgij', cp_im[:S5_L], bb_im, precision=hp))
    eye_g = jnp.eye(S5_GROUPS, dtype=F32)
    kbd = (jnp.transpose(kd, (0, 1, 3, 2))[:, :, :, None, :] * eye_g[None, :, None, :, None])
    kbd = kbd.reshape(S5_L, D_BRANCH, D_BRANCH)

    eye_h = jnp.eye(S5_HALF_G, dtype=F32)

    def halves(t):
        return jnp.moveaxis(t.reshape((S5_L, 2, S5_HALF_G) + t.shape[2:]), 1, 0)

    bbt_re = jnp.swapaxes(bb_re, 1, 2)[None]
    bbt_im = jnp.swapaxes(bb_im, 1, 2)[None]
    rr, ri_ = rev_re[:, :, None, :], rev_im[:, :, None, :]
    wbt = jnp.stack([halves(rr * bbt_re - ri_ * bbt_im),
                     halves(rr * bbt_im + ri_ * bbt_re)], axis=4)
    wst = wbt[:, :, :, :, :, None, :] * eye_h[None, None, :, None, None, :, None]
    wst = wst.reshape(2, S5_L, 128, 2 * S5_HALF_STATE)
    cpn = jnp.stack([halves(cp_re[1:S5_L + 1]), halves(-cp_im[1:S5_L + 1])], axis=4)
    wout = cpn[:, :, :, :, :, None, :] * eye_h[None, None, :, None, None, :, None]
    wout = wout.reshape(2, S5_L, 128, 2 * S5_HALF_STATE)
    lam_l = jnp.stack([pw_re[S5_L].reshape(2, S5_HALF_STATE),
                       pw_im[S5_L].reshape(2, S5_HALF_STATE)], axis=1)
    return kbd.astype(BF16), wst.astype(BF16), wout.astype(BF16), lam_l


def _s5(proj, a_re, a_im, b_re, b_im, c_re, c_im, d_skip, log_dt, w_glu, batch, seq):
    kbd, wst, wout, lam_l = _s5_weights(a_re, a_im, b_re, b_im, c_re, c_im, log_dt)
    nchunk = seq // S5_L
    full = lambda shape: pl.BlockSpec(shape, lambda b: (0,) * len(shape))
    return pl.pallas_call(
        _s5_kernel,
        grid=(batch,),
        in_specs=[
            pl.BlockSpec((seq, 128), lambda b: (b, COL_U // 128)),
            pl.BlockSpec((seq, 128), lambda b: (b, COL_U // 128 + 1)),
            full((S5_L, D_BRANCH, D_BRANCH)),
            full((2, S5_L, 128, 2 * S5_HALF_STATE)),
            full((2, S5_L, 128, 2 * S5_HALF_STATE)),
            full((2, 2, S5_HALF_STATE)),
            full((1, D_BRANCH)),
            full((D_BRANCH, D_BRANCH)),
        ],
        out_specs=pl.BlockSpec((seq, D_BRANCH), lambda b: (b, 0)),
        out_shape=jax.ShapeDtypeStruct((batch * seq, D_BRANCH), F32),
        scratch_shapes=[
            pltpu.VMEM((2, nchunk, 2 * S5_HALF_STATE), F32),
            pltpu.VMEM((2, nchunk, 2 * S5_HALF_STATE), F32),
            pltpu.VMEM((seq, 128), F32),
            pltpu.VMEM((seq, 128), F32),
        ],
        compiler_params=_cparams(),
        name="s5",
    )(proj, proj, kbd, wst, wout, lam_l, d_skip.reshape(1, D_BRANCH), w_glu.astype(BF16))


def _attn_kernel(q_ref, k_ref, v_ref, qg_ref, kg_ref, bias_ref, onesbd_ref, hm_ref, o_ref,
                 kp_s, vp_s):
    seq = q_ref.shape[0]
    kp_s[0:ATT_PAD, :] = jnp.zeros((ATT_PAD, D_BRANCH), BF16)
    vp_s[0:ATT_PAD, :] = jnp.zeros((ATT_PAD, D_BRANCH), BF16)
    rt = 256

    def prep(i, carry):
        rows = pl.ds(pl.multiple_of(i * rt, rt), rt)
        k = k_ref[rows, :]
        ms = _dot_exact_rhs(k * k, onesbd_ref[...]) * (1.0 / ATT_HEADDIM)
        kn = k * lax.rsqrt(ms + EPS) * kg_ref[...]
        prow = pl.ds(pl.multiple_of(ATT_PAD + i * rt, 64), rt)
        kp_s[prow, :] = kn.astype(BF16)
        vp_s[prow, :] = v_ref[rows, :].astype(BF16)
        return carry

    lax.fori_loop(0, seq // rt, prep, 0)

    pidx = lax.broadcasted_iota(jnp.int32, (ATT_HEADS * CHUNK, ATT_BAND), 1)
    scale = ATT_HEADDIM ** -0.5

    def chunk(c, carry):
        r0 = pl.multiple_of(c * CHUNK, CHUNK)
        q = q_ref[pl.ds(r0, CHUNK), :]
        ms = _dot_exact_rhs(q * q, onesbd_ref[...]) * (1.0 / ATT_HEADDIM)
        qn = q * lax.rsqrt(ms + EPS) * (qg_ref[...] * scale)
        qs = jnp.concatenate([qn * hm_ref[h:h + 1, :] for h in range(ATT_HEADS)], axis=0).astype(BF16)
        kb = kp_s[pl.ds(r0, ATT_BAND), :]
        vb = vp_s[pl.ds(r0, ATT_BAND), :]
        s = _dot_nt(qs, kb)
        first_valid = jnp.maximum(ATT_LEFT_CHUNKS + 1 - c, 1) * CHUNK
        s = jnp.where(pidx >= first_valid, s + bias_ref[...], NEG_INF)
        m = jnp.max(s, axis=-1, keepdims=True)
        e = jnp.exp(s - m)
        l = jnp.sum(e, axis=-1, keepdims=True)
        pv = _dot(e.astype(BF16), vb) / l
        out = pv[0:CHUNK, :] * hm_ref[0:1, :]
        for h in range(1, ATT_HEADS):
            out = out + pv[h * CHUNK:(h + 1) * CHUNK, :] * hm_ref[h:h + 1, :]
        o_ref[pl.ds(r0, CHUNK), :] = out
        return carry

    lax.fori_loop(0, seq // CHUNK, chunk, 0, unroll=2)


def _attn(proj, q_gain, k_gain, rel_bias, batch, seq):
    ndiag = CHUNK + ATT_BAND
    m = np.arange(ndiag)
    rel = np.clip(m - ATT_BAND + ATT_LEFT_CHUNKS * CHUNK + CHUNK, -ATT_MAX_REL, ATT_MAX_REL) + ATT_MAX_REL
    diag = rel_bias[:, rel[::-1].copy()]
    bias = jnp.stack([diag[:, CHUNK - 1 - t:CHUNK - 1 - t + ATT_BAND] for t in range(CHUNK)], axis=1)
    bias = bias.reshape(ATT_HEADS * CHUNK, ATT_BAND)
    full = lambda shape: pl.BlockSpec(shape, lambda b: (0,) * len(shape))
    colblk = lambda j: pl.BlockSpec((seq, D_BRANCH), lambda b: (b, COL_QKV // D_BRANCH + j))
    return pl.pallas_call(
        _attn_kernel,
        grid=(batch,),
        in_specs=[
            colblk(0), colblk(1), colblk(2),
            full((1, D_BRANCH)), full((1, D_BRANCH)),
            full((ATT_HEADS * CHUNK, ATT_BAND)),
            full((D_BRANCH, D_BRANCH)), full((4, D_BRANCH)),
        ],
        out_specs=pl.BlockSpec((seq, D_BRANCH), lambda b: (b, 0)),
        out_shape=jax.ShapeDtypeStruct((batch * seq, D_BRANCH), F32),
        scratch_shapes=[
            pltpu.VMEM((ATT_PAD + seq, D_BRANCH), BF16),
            pltpu.VMEM((ATT_PAD + seq, D_BRANCH), BF16),
        ],
        compiler_params=_cparams(),
        name="attn",
    )(proj, proj, proj, jnp.tile(q_gain, ATT_HEADS).reshape(1, D_BRANCH),
      jnp.tile(k_gain, ATT_HEADS).reshape(1, D_BRANCH), bias, _ones_bd(64), _head_masks())


def kernel(x, p, norm_mix, w_in, hg_lb_logits, hg_o_norm, ssd_conv_w, ssd_conv_b, ssd_dt_bias, ssd_A_log, ssd_D, ssd_norm, s5_A_re, s5_A_im, s5_B_re, s5_B_im, s5_C_re, s5_C_im, s5_D, s5_log_dt, s5_w_glu, att_q_norm, att_k_norm, att_rel_bias, w_branch, w_out, norm_ffn, w_ff1, w_ff2, w_ple, norm_ple, w_ple_gate):
    batch, seq, _ = x.shape
    depth = w_in.shape[0]
    t = batch * seq
    lb_all = jnp.cumsum(jax.nn.softmax(hg_lb_logits.astype(F32), axis=0), axis=0)
    lb_all = lb_all - lb_all[0:1]

    x2 = x.reshape(t, D_MODEL)
    p3 = p.reshape(depth, t, D_PLE)
    w_mix, w_gate = _repack(w_in)
    for i in range(depth):
        g_mix = norm_mix[i].reshape(1, D_MODEL)

        proj = _inproj(x2, g_mix, w_mix, i)
        y_a = _hgrn(proj, lb_all[i], hg_o_norm[i], batch, seq)
        y_b = _ssd(proj, ssd_conv_w[i], ssd_conv_b[i], ssd_dt_bias[i], ssd_A_log[i], ssd_D[i],
                   ssd_norm[i], batch, seq)
        y_c = _s5(proj, s5_A_re[i], s5_A_im[i], s5_B_re[i], s5_B_im[i], s5_C_re[i], s5_C_im[i],
                  s5_D[i], s5_log_dt[i], s5_w_glu[i], batch, seq)
        y_d = _attn(proj, att_q_norm[i], att_k_norm[i], att_rel_bias[i], batch, seq)

        x2 = _merge(x2, y_a, y_b, y_c, y_d, g_mix, w_gate, w_branch[i].astype(BF16),
                    w_out[i].astype(BF16), i)
        x2 = _ffn(x2, norm_ffn[i].reshape(1, D_MODEL), w_ff1[i].astype(BF16), w_ff2[i].astype(BF16))
        x2 = _ple(x2, p3, norm_ple[i].reshape(1, D_MODEL), w_ple_gate[i].astype(BF16),
                  w_ple[i].astype(BF16), i)
    return x2.reshape(batch, seq, D_MODEL)
```

```python
import math

import jax
import jax.numpy as jnp
import numpy as np
from jax import lax
from jax.experimental import pallas as pl
from jax.experimental.pallas import tpu as pltpu

F32 = jnp.float32
BF16 = jnp.bfloat16

D_MODEL = 1024
CHUNK = 64
D_PLE = 256
N_BRANCH = 4
D_BRANCH = 256
D_FF = 4096
EPS = 1e-6
NEG_INF = -1e30
LOG2E = math.log2(math.e)

HG_HEADS = 4
HG_DK = 64
SSD_HEADS = 4
SSD_HEADDIM = 64
SSD_GROUPS = 2
SSD_DSTATE = 64
SSD_CONV = 4
SSD_INNER = 256
SSD_XBC = 512
S5_GROUP_CH = 16
S5_GROUPS = 16
S5_STATE = 64
ATT_HEADS = 4
ATT_HEADDIM = 64
ATT_LEFT_CHUNKS = 8
ATT_MAX_REL = 128

COL_HG = 0
COL_XBC = 1024
COL_Z = 1536
COL_U = 1792
COL_QKV = 2048
COL_DT = 2816
N_MIX = 2944
DT_PAD = 128

W_HG, W_Z, W_XBC, W_DT, W_U, W_QKV, W_GATE = 0, 1024, 1280, 1792, 1796, 2052, 2820
N_IN = 6916

S5_L = 8
S5_HALF_G = 8
S5_HALF_STATE = S5_HALF_G * S5_STATE

ATT_BAND = (ATT_LEFT_CHUNKS + 2) * CHUNK
ATT_PAD = (ATT_LEFT_CHUNKS + 1) * CHUNK

VMEM_LIMIT = 56 * 1024 * 1024


def _cparams(n_axes=1):
    return pltpu.CompilerParams(
        dimension_semantics=("arbitrary",) * n_axes, vmem_limit_bytes=VMEM_LIMIT)


def _layer_spec(shape, layer):
    return pl.BlockSpec((None,) + tuple(shape), lambda *_: (layer,) + (0,) * len(shape),
                        pipeline_mode=pl.Buffered(1))


def _const_spec(shape):
    return pl.BlockSpec(tuple(shape), lambda *_: (0,) * len(shape), pipeline_mode=pl.Buffered(1))


def _dot(a, b):
    return jnp.dot(a, b, preferred_element_type=F32)


def _dot_nt(a, b):
    return lax.dot_general(a, b, (((1,), (1,)), ((), ())), preferred_element_type=F32)


def _dot_tn(a, b):
    return lax.dot_general(a, b, (((0,), (0,)), ((), ())), preferred_element_type=F32)


def _split3(a):
    a1 = a.astype(BF16)
    r1 = a - a1.astype(F32)
    a2 = r1.astype(BF16)
    r2 = r1 - a2.astype(F32)
    return a1, a2, r2.astype(BF16)


def _dot_exact_rhs(a, m):
    a1, a2, a3 = _split3(a)
    return _dot(a1, m) + _dot(a2, m) + _dot(a3, m)


def _dot_exact_lhs(m, a):
    a1, a2, a3 = _split3(a)
    return _dot(m, a1) + _dot(m, a2) + _dot(m, a3)


def _sigmoid(x):
    return 1.0 / (1.0 + jnp.exp(-x))


def _softplus(x):
    return jnp.maximum(x, 0.0) + jnp.log1p(jnp.exp(-jnp.abs(x)))


def _rms_rows(x, g):
    return x * lax.rsqrt(jnp.mean(x * x, axis=-1, keepdims=True) + EPS) * g


def _repack_kernel(w_ref, mix_ref, gate_ref):
    def piece(lo, hi):
        return w_ref[:, lo:hi].astype(BF16)

    mix_ref[:, COL_HG:COL_HG + 1024] = piece(W_HG, W_HG + 1024)
    mix_ref[:, COL_XBC:COL_XBC + SSD_XBC] = piece(W_XBC, W_XBC + SSD_XBC)
    mix_ref[:, COL_Z:COL_Z + SSD_INNER] = piece(W_Z, W_Z + SSD_INNER)
    mix_ref[:, COL_U:COL_U + D_BRANCH] = piece(W_U, W_U + D_BRANCH)
    mix_ref[:, COL_QKV:COL_QKV + 3 * D_BRANCH] = piece(W_QKV, W_QKV + 3 * D_BRANCH)
    lane = lax.broadcasted_iota(jnp.int32, (w_ref.shape[0], DT_PAD), 1)
    dt = jnp.where(lane < SSD_HEADS, w_ref[:, W_DT:W_DT + DT_PAD], 0.0)
    mix_ref[:, COL_DT:COL_DT + DT_PAD] = dt.astype(BF16)
    gate_ref[...] = piece(W_GATE, N_IN)


def _repack(w_in, tk=128):
    depth = w_in.shape[0]
    return pl.pallas_call(
        _repack_kernel,
        grid=(depth, D_MODEL // tk),
        in_specs=[pl.BlockSpec((None, tk, N_IN), lambda l, r: (l, r, 0))],
        out_specs=[pl.BlockSpec((None, tk, N_MIX), lambda l, r: (l, r, 0)),
                   pl.BlockSpec((None, tk, N_BRANCH * D_MODEL), lambda l, r: (l, r, 0))],
        out_shape=[jax.ShapeDtypeStruct((depth, D_MODEL, N_MIX), BF16),
                   jax.ShapeDtypeStruct((depth, D_MODEL, N_BRANCH * D_MODEL), BF16)],
        compiler_params=_cparams(2),
        name="repack",
    )(w_in)


def _inproj_kernel(x_ref, g_ref, w_ref, o_ref):
    h = _rms_rows(x_ref[...], g_ref[...])
    o_ref[...] = _dot(h.astype(BF16), w_ref[...])


def _inproj(x2, g, w_mix, layer, tm=512):
    t = x2.shape[0]
    return pl.pallas_call(
        _inproj_kernel,
        grid=(t // tm,),
        in_specs=[
            pl.BlockSpec((tm, D_MODEL), lambda i: (i, 0)),
            _layer_spec((1, D_MODEL), layer),
            _layer_spec((D_MODEL, N_MIX), layer),
        ],
        out_specs=pl.BlockSpec((tm, N_MIX), lambda i: (i, 0)),
        out_shape=jax.ShapeDtypeStruct((t, N_MIX), F32),
        compiler_params=_cparams(),
        name="inproj",
    )(x2, g, w_mix)


def _merge_kernel(x_ref, ya_ref, yb_ref, yc_ref, yd_ref, g_ref, wg_ref, wb_ref, wo_ref, o_ref):
    x = x_ref[...]
    h = _rms_rows(x, g_ref[...]).astype(BF16)
    merged = None
    for m, y_ref in enumerate((ya_ref, yb_ref, yc_ref, yd_ref)):
        gate = _sigmoid(_dot(h, wg_ref[:, m * D_MODEL:(m + 1) * D_MODEL]))
        term = gate * _dot(y_ref[...].astype(BF16), wb_ref[m])
        merged = term if merged is None else merged + term
    o_ref[...] = x + _dot(merged.astype(BF16), wo_ref[...])


def _merge(x2, ya, yb, yc, yd, g, w_gate, w_branch, w_out, layer, tm=512):
    t = x2.shape[0]
    row = lambda i: (i, 0)
    return pl.pallas_call(
        _merge_kernel,
        grid=(t // tm,),
        in_specs=[
            pl.BlockSpec((tm, D_MODEL), row),
            pl.BlockSpec((tm, D_BRANCH), row),
            pl.BlockSpec((tm, D_BRANCH), row),
            pl.BlockSpec((tm, D_BRANCH), row),
            pl.BlockSpec((tm, D_BRANCH), row),
            _layer_spec((1, D_MODEL), layer),
            _layer_spec((D_MODEL, N_BRANCH * D_MODEL), layer),
            _layer_spec((N_BRANCH, D_BRANCH, D_MODEL), layer),
            _layer_spec((D_MODEL, D_MODEL), layer),
        ],
        out_specs=pl.BlockSpec((tm, D_MODEL), row),
        out_shape=jax.ShapeDtypeStruct((t, D_MODEL), F32),
        compiler_params=_cparams(),
        name="merge",
    )(x2, ya, yb, yc, yd, g, w_gate, w_branch, w_out)


def _ffn_ple_kernel(x_ref, p_ref, g1_ref, w1_ref, w2_ref, g2_ref, wg_ref, wp_ref, o_ref):
    x = x_ref[...]
    h = _rms_rows(x, g1_ref[...]).astype(BF16)
    acc = x
    tk = 1024
    for j in range(D_FF // tk):
        a = jnp.maximum(_dot(h, w1_ref[:, j * tk:(j + 1) * tk]), 0.0)
        acc = acc + _dot((a * a).astype(BF16), w2_ref[j * tk:(j + 1) * tk, :])
    h2 = _rms_rows(acc, g2_ref[...]).astype(BF16)
    gate = _sigmoid(_dot(h2, wg_ref[...]))
    o_ref[...] = acc + _dot(p_ref[...].astype(BF16), wp_ref[...]) * gate


def _ffn_ple(x2, p3, g_ffn, w1, w2, g_ple, w_gate, w_ple, layer, tm=512):
    t = x2.shape[0]
    return pl.pallas_call(
        _ffn_ple_kernel,
        grid=(t // tm,),
        in_specs=[
            pl.BlockSpec((tm, D_MODEL), lambda i: (i, 0)),
            pl.BlockSpec((None, tm, D_PLE), lambda i: (layer, i, 0)),
            _layer_spec((1, D_MODEL), layer),
            _layer_spec((D_MODEL, D_FF), layer),
            _layer_spec((D_FF, D_MODEL), layer),
            _layer_spec((1, D_MODEL), layer),
            _layer_spec((D_MODEL, D_MODEL), layer),
            _layer_spec((D_PLE, D_MODEL), layer),
        ],
        out_specs=pl.BlockSpec((tm, D_MODEL), lambda i: (i, 0)),
        out_shape=jax.ShapeDtypeStruct((t, D_MODEL), F32),
        compiler_params=_cparams(),
        name="ffn_ple",
    )(x2, p3, g_ffn, w1, w2, g_ple, w_gate, w_ple)


def _tri64():
    i = np.arange(CHUNK)
    return jnp.asarray(i[:, None] >= i[None, :], BF16)


def _ones_bd(block, n=D_BRANCH):
    i = np.arange(n) // block
    return jnp.asarray(i[:, None] == i[None, :], BF16)


def _head_masks():
    lane = np.arange(D_BRANCH) // 64
    return jnp.asarray(lane[None, :] == np.arange(4)[:, None], F32)


HG_SUB = 16
HG_TILE = 256
HG_SUB_STACK = 8 * HG_SUB + 4 * HG_SUB


def _hgrn_kernel(p_ref, vec_ref, tri_ref, onesbd_ref, bdmask_ref, o_ref, st_ref, bl_s, ck_s, w_s):
    seq = p_ref.shape[0]
    half = HG_SUB // 2
    loglb = vec_ref[0:1, :]
    log1mlb = vec_ref[1:2, :]
    gain = vec_ref[2:3, :]

    def pre(i, carry):
        rows = pl.ds(pl.multiple_of(i * HG_TILE, HG_TILE), HG_TILE)
        z = p_ref[rows, 256:512]
        ls = jnp.minimum(z, 0.0) - jnp.log1p(jnp.exp(-jnp.abs(z)))
        y = log1mlb + ls
        lf = jnp.maximum(loglb, y) + jnp.log1p(jnp.exp(-jnp.abs(loglb - y)))
        bl2 = _dot_exact_lhs(tri_ref[...], lf) * LOG2E
        bl_s[rows, :] = bl2
        ck_s[rows, :] = bl2 - (log1mlb + (ls - z)) * LOG2E
        return carry

    lax.fori_loop(0, seq // HG_TILE, pre, 0)

    row16 = lax.broadcasted_iota(jnp.int32, (HG_SUB, D_BRANCH), 0)
    row8 = lax.broadcasted_iota(jnp.int32, (half, D_BRANCH), 0)

    def chunk(c, carry):
        r0 = c * CHUNK
        subs = []
        for s in range(CHUNK // HG_SUB):
            rs = pl.multiple_of(r0 + s * HG_SUB, HG_SUB)
            rows = pl.ds(rs, HG_SUB)
            q = p_ref[rows, 0:256]
            bl = bl_s[rows, :]
            base = s * HG_SUB_STACK
            for j in range(half):
                d = jnp.where(row16 >= j, bl - ck_s[pl.ds(rs + j, 1), :], NEG_INF)
                w_s[base + j * HG_SUB:base + (j + 1) * HG_SUB, :] = (q * jnp.exp2(d)).astype(BF16)
            q_hi = q[half:, :]
            bl_hi = bl[half:, :]
            for jp in range(half // 2):
                pieces = []
                for j in (half + 2 * jp, half + 2 * jp + 1):
                    d = jnp.where(row8 >= j - half, bl_hi - ck_s[pl.ds(rs + j, 1), :], NEG_INF)
                    pieces.append(q_hi * jnp.exp2(d))
                off = base + half * HG_SUB + jp * HG_SUB
                w_s[off:off + HG_SUB, :] = jnp.concatenate(pieces, axis=0).astype(BF16)
            subs.append((rs, rows, q, bl))

        r = _dot(w_s[...], onesbd_ref[...])

        for s, (rs, rows, q, bl) in enumerate(subs):
            base = s * HG_SUB_STACK
            o_lo = jnp.zeros((half, D_BRANCH), F32)
            o_hi = jnp.zeros((half, D_BRANCH), F32)
            for j in range(HG_SUB):
                vj = p_ref[pl.ds(rs + j, 1), 512:768]
                if j < half:
                    o_lo = o_lo + r[base + j * HG_SUB:base + j * HG_SUB + half, :] * vj
                    o_hi = o_hi + r[base + j * HG_SUB + half:base + (j + 1) * HG_SUB, :] * vj
                else:
                    off = base + half * HG_SUB + (j - half) * half
                    o_hi = o_hi + r[off:off + half, :] * vj
            o_ref[rows, :] = jnp.concatenate([o_lo, o_hi], axis=0)
        return carry

    lax.fori_loop(0, seq // CHUNK, chunk, 0, unroll=2)

    st_ref[...] = jnp.zeros_like(st_ref)
    nsub = CHUNK // HG_SUB

    def carry_state(c, carry):
        rows = pl.ds(pl.multiple_of(c * CHUNK, CHUNK), CHUNK)
        q = p_ref[rows, 0:256]
        v = p_ref[rows, 512:768].astype(BF16)
        b = bl_s[rows, :]
        ck = ck_s[rows, :]
        ends = [b[(s + 1) * HG_SUB - 1:(s + 1) * HG_SUB, :] for s in range(nsub)]
        st = st_ref[...]
        o = _dot_nt((q * jnp.exp2(b)).astype(BF16), st.astype(BF16))
        st = st * jnp.exp2(ends[-1])
        for s in range(nsub):
            sub = slice(s * HG_SUB, (s + 1) * HG_SUB)
            ka = jnp.exp2(ends[s] - ck[sub, :]).astype(BF16)
            upd = _dot_tn(v[sub, :], ka) * bdmask_ref[...]
            if s + 1 < nsub:
                after = slice((s + 1) * HG_SUB, CHUNK)
                lhs = (q[after, :] * jnp.exp2(b[after, :] - ends[s])).astype(BF16)
                o = jnp.concatenate([o[:(s + 1) * HG_SUB, :],
                                     o[after, :] + _dot_nt(lhs, upd.astype(BF16))], axis=0)
                st = st + upd * jnp.exp2(ends[-1] - ends[s])
            else:
                st = st + upd
        o_ref[rows, :] = o_ref[rows, :] + o
        st_ref[...] = st
        return carry

    lax.fori_loop(0, seq // CHUNK, carry_state, 0, unroll=2)

    def post(i, carry):
        rows = pl.ds(pl.multiple_of(i * HG_TILE, HG_TILE), HG_TILE)
        o = o_ref[rows, :]
        g = p_ref[rows, 768:1024]
        ms = _dot_exact_rhs(o * o, onesbd_ref[...]) * (1.0 / HG_DK)
        o_ref[rows, :] = o * lax.rsqrt(ms + EPS) * gain * (g * _sigmoid(g))
        return carry

    lax.fori_loop(0, seq // HG_TILE, post, 0)


def _hgrn(proj, hg_vec, layer, batch, seq):
    bd = _ones_bd(64)
    i = np.arange(HG_TILE)
    tri = jnp.asarray((i[:, None] // CHUNK == i[None, :] // CHUNK) & (i[:, None] >= i[None, :]), BF16)
    return pl.pallas_call(
        _hgrn_kernel,
        grid=(batch,),
        in_specs=[
            pl.BlockSpec((seq, 1024), lambda b: (b, COL_HG // 1024)),
            _layer_spec((3, D_BRANCH), layer),
            _const_spec((HG_TILE, HG_TILE)), _const_spec((D_BRANCH, D_BRANCH)),
            _const_spec((D_BRANCH, D_BRANCH)),
        ],
        out_specs=pl.BlockSpec((seq, D_BRANCH), lambda b: (b, 0)),
        out_shape=jax.ShapeDtypeStruct((batch * seq, D_BRANCH), F32),
        scratch_shapes=[
            pltpu.VMEM((D_BRANCH, D_BRANCH), F32),
            pltpu.VMEM((seq, D_BRANCH), F32),
            pltpu.VMEM((seq, D_BRANCH), F32),
            pltpu.VMEM((CHUNK // HG_SUB * HG_SUB_STACK, D_BRANCH), BF16),
        ],
        compiler_params=_cparams(),
        name="hgrn",
    )(proj, hg_vec, tri, bd, bd.astype(F32))


def _ssd_kernel(xbc_ref, z_ref, dt_ref, cw_ref, cb_ref, v128_ref, v256_ref,
                tri_ref, e_ref, u_ref, gm_ref, hm_ref, o_ref, sn_ref):
    seq = xbc_ref.shape[0]
    dtb_ref = v128_ref.at[0:1, :]
    alog_ref = v128_ref.at[1:2, :]
    dx_ref = v256_ref.at[0:1, :]
    ng_ref = v256_ref.at[1:2, :]
    sn_ref[...] = jnp.zeros_like(sn_ref)
    lidx = lax.broadcasted_iota(jnp.int32, (CHUNK, D_BRANCH), 0)
    sidx = lax.broadcasted_iota(jnp.int32, (CHUNK, D_BRANCH), 1) % CHUNK
    causal = lidx >= sidx
    lane128 = lax.broadcasted_iota(jnp.int32, (1, 2 * SSD_DSTATE), 1)
    a_neg = -jnp.exp(alog_ref[...])

    def chunk(c, carry):
        r0 = pl.multiple_of(c * CHUNK, CHUNK)
        rows = pl.ds(r0, CHUNK)
        cur = xbc_ref[rows, :]
        prev = xbc_ref[pl.ds(pl.multiple_of(jnp.maximum(r0 - 8, 0), 8), 8), :]
        prev = prev * jnp.where(c > 0, 1.0, 0.0)
        ext = jnp.concatenate([prev, cur], axis=0)
        conv = cb_ref[...] + cur * cw_ref[SSD_CONV - 1:SSD_CONV, :]
        for s in range(1, SSD_CONV):
            sh = pltpu.roll(ext, s, axis=0)[8:8 + CHUNK, :]
            conv = conv + sh * cw_ref[SSD_CONV - 1 - s:SSD_CONV - s, :]
        xa = conv * _sigmoid(conv)
        xs = xa[:, 0:SSD_INNER]
        bm = xa[:, SSD_INNER:SSD_INNER + 128]
        cm = xa[:, SSD_INNER + 128:SSD_INNER + 256]

        dt = _softplus(dt_ref[rows, :] + dtb_ref[...])
        dtx = _dot_exact_rhs(dt, e_ref[...])
        adtx = _dot_exact_rhs(dt * a_neg, e_ref[...])
        acum = _dot_exact_lhs(tri_ref[...], adtx)
        seg = _dot_exact_lhs(tri_ref[...], adtx * u_ref[...])
        lmat = jnp.exp(jnp.where(causal, seg, NEG_INF))
        xdt = xs * dtx

        bm16 = bm.astype(BF16)
        y = jnp.zeros((CHUNK, D_BRANCH), F32)
        for grp in range(SSD_GROUPS):
            gsel = (lane128 // SSD_DSTATE) == grp
            gmat = _dot_nt(jnp.where(gsel, cm, 0.0).astype(BF16), bm16)
            for h in range(grp * 2, grp * 2 + 2):
                mh = gmat * lmat[:, h * CHUNK:(h + 1) * CHUNK]
                y = y + _dot(mh.astype(BF16), (xdt * hm_ref[h:h + 1, :]).astype(BF16))

        sn_old = sn_ref[...]
        y = y + _dot(cm.astype(BF16), sn_old.astype(BF16)) * jnp.exp(acum)
        aend = acum[CHUNK - 1:CHUNK, :]
        upd = _dot_tn(bm16, (jnp.exp(aend - acum) * xdt).astype(BF16))
        sn_ref[...] = jnp.exp(aend) * sn_old + upd * gm_ref[...]

        y = y + dx_ref[...] * xs
        zz = z_ref[rows, :]
        y = y * (zz * _sigmoid(zz))
        halves = []
        for grp in range(SSD_GROUPS):
            yh = y[:, grp * 128:(grp + 1) * 128]
            halves.append(yh * lax.rsqrt(jnp.mean(yh * yh, axis=-1, keepdims=True) + EPS))
        o_ref[rows, :] = jnp.concatenate(halves, axis=1) * ng_ref[...]
        return carry

    lax.fori_loop(0, seq // CHUNK, chunk, 0, unroll=4)


def _ssd(proj, conv_wt, conv_b, v128, v256, layer, batch, seq):
    full = _const_spec
    e = np.zeros((DT_PAD, D_BRANCH), np.float32)
    for h in range(SSD_HEADS):
        e[h, h * 64:(h + 1) * 64] = 1.0
    li = np.arange(CHUNK)
    u_t = np.tile((li[:, None] > li[None, :]).astype(np.float32), (1, SSD_HEADS))
    gm = ((np.arange(128) // 64)[:, None] == (np.arange(256) // 128)[None, :]).astype(np.float32)
    return pl.pallas_call(
        _ssd_kernel,
        grid=(batch,),
        in_specs=[
            pl.BlockSpec((seq, SSD_XBC), lambda b: (b, COL_XBC // SSD_XBC)),
            pl.BlockSpec((seq, SSD_INNER), lambda b: (b, COL_Z // SSD_INNER)),
            pl.BlockSpec((seq, DT_PAD), lambda b: (b, COL_DT // DT_PAD)),
            _layer_spec((SSD_CONV, SSD_XBC), layer), _layer_spec((1, SSD_XBC), layer),
            _layer_spec((2, DT_PAD), layer), _layer_spec((2, D_BRANCH), layer),
            full((CHUNK, CHUNK)), full((DT_PAD, D_BRANCH)), full((CHUNK, D_BRANCH)),
            full((128, D_BRANCH)), full((4, D_BRANCH)),
        ],
        out_specs=pl.BlockSpec((seq, D_BRANCH), lambda b: (b, 0)),
        out_shape=jax.ShapeDtypeStruct((batch * seq, D_BRANCH), F32),
        scratch_shapes=[pltpu.VMEM((2 * SSD_DSTATE, D_BRANCH), F32)],
        compiler_params=_cparams(),
        name="ssd",
    )(proj, proj, proj, conv_wt, conv_b, v128, v256,
      _tri64(), jnp.asarray(e, BF16), jnp.asarray(u_t), jnp.asarray(gm), _head_masks())


S5_RT = 256


def _gelu_tanh(x):
    return 0.5 * x * (1.0 + jnp.tanh(math.sqrt(2.0 / math.pi) * (x + 0.044715 * (x * x * x))))


def _s5_kernel(ua_ref, ub_ref, kbd_ref, wst_ref, wout_ref, lam_ref, d_ref, wglu_ref, o_ref,
               hloc_s, hprev_s, ya_s, yb_s):
    u_refs = (ua_ref, ub_ref)
    y_refs = (ya_s, yb_s)
    seq = ua_ref.shape[0]
    nchunk = seq // S5_L
    hs = S5_HALF_STATE
    rowmod = lax.broadcasted_iota(jnp.int32, (S5_RT, D_BRANCH), 0) % S5_L

    def tile(i, carry):
        rows = pl.ds(pl.multiple_of(i * S5_RT, S5_RT), S5_RT)
        ut = jnp.concatenate([ua_ref[rows, :], ub_ref[rows, :]], axis=1)
        acc = _dot(ut.astype(BF16), kbd_ref[0])
        for d in range(1, S5_L):
            ud = jnp.where(rowmod >= d, pltpu.roll(ut, d, axis=0), 0.0)
            acc = acc + _dot(ud.astype(BF16), kbd_ref[d])
        ya_s[rows, :] = acc[:, 0:128]
        yb_s[rows, :] = acc[:, 128:256]
        return carry

    lax.fori_loop(0, seq // S5_RT, tile, 0)

    for half in range(2):
        acc = None
        for s in range(S5_L):
            us = u_refs[half][pl.ds(s, nchunk, stride=S5_L), :]
            term = _dot(us.astype(BF16), wst_ref[half, s])
            acc = term if acc is None else acc + term
        hloc_s[half] = acc

    for half in range(2):
        lr = lam_ref[half, 0:1, :]
        li = lam_ref[half, 1:2, :]

        def step(c, carry):
            hr, hi = carry
            hprev_s[half, pl.ds(c, 1), 0:hs] = hr
            hprev_s[half, pl.ds(c, 1), hs:2 * hs] = hi
            loc = hloc_s[half, pl.ds(c, 1), :]
            return (lr * hr - li * hi + loc[:, 0:hs], lr * hi + li * hr + loc[:, hs:2 * hs])

        zero = jnp.zeros((1, hs), F32)
        lax.fori_loop(0, nchunk, step, (zero, zero))

    for half in range(2):
        hp = hprev_s[half].astype(BF16)
        for s in range(S5_L):
            srows = pl.ds(s, nchunk, stride=S5_L)
            y_refs[half][srows, :] = y_refs[half][srows, :] + _dot_nt(hp, wout_ref[half, s])

    def tail(i, carry):
        rows = pl.ds(pl.multiple_of(i * S5_RT, S5_RT), S5_RT)
        ut = jnp.concatenate([ua_ref[rows, :], ub_ref[rows, :]], axis=1)
        yt = jnp.concatenate([ya_s[rows, :], yb_s[rows, :]], axis=1)
        y = _gelu_tanh(yt + d_ref[...] * ut)
        o_ref[rows, :] = y * _sigmoid(_dot(y.astype(BF16), wglu_ref[...]))
        return carry

    lax.fori_loop(0, seq // S5_RT, tail, 0)


def _s5_weights(a_re, a_im, b_re, b_im, c_re, c_im, log_dt):
    hp = lax.Precision.HIGHEST
    step = jnp.exp(log_dt)[:, None]
    mag = jnp.exp(a_re * step)
    lam_re = mag * jnp.cos(a_im * step)
    lam_im = mag * jnp.sin(a_im * step)
    den = a_re * a_re + a_im * a_im
    num_re = lam_re - 1.0
    coef_re = (num_re * a_re + lam_im * a_im) / den
    coef_im = (lam_im * a_re - num_re * a_im) / den
    bb_re = coef_re[..., None] * b_re - coef_im[..., None] * b_im
    bb_im = coef_re[..., None] * b_im + coef_im[..., None] * b_re

    pw_re = [jnp.ones_like(lam_re)]
    pw_im = [jnp.zeros_like(lam_im)]
    for _ in range(S5_L):
        pr, pi = pw_re[-1], pw_im[-1]
        pw_re.append(pr * lam_re - pi * lam_im)
        pw_im.append(pr * lam_im + pi * lam_re)
    rev_re = jnp.stack(pw_re[S5_L - 1::-1])
    rev_im = jnp.stack(pw_im[S5_L - 1::-1])
    pw_re = jnp.stack(pw_re)
    pw_im = jnp.stack(pw_im)

    cp_re = c_re[None] * pw_re[:, :, None, :] - c_im[None] * pw_im[:, :, None, :]
    cp_im = c_re[None] * pw_im[:, :, None, :] + c_im[None] * pw_re[:, :, None, :]
    kd = (jnp.einsum('dgip,gpj->dgij', cp_re[:S5_L], bb_re, precision=hp)
          - jnp.einsum('dgip,gpj->dgij', cp_im[:S5_L], bb_im, precision=hp))
    eye_g = jnp.eye(S5_GROUPS, dtype=F32)
    kbd = (jnp.transpose(kd, (0, 1, 3, 2))[:, :, :, None, :] * eye_g[None, :, None, :, None])
    kbd = kbd.reshape(S5_L, D_BRANCH, D_BRANCH)

    eye_h = jnp.eye(S5_HALF_G, dtype=F32)

    def halves(t):
        return jnp.moveaxis(t.reshape((S5_L, 2, S5_HALF_G) + t.shape[2:]), 1, 0)

    bbt_re = jnp.swapaxes(bb_re, 1, 2)[None]
    bbt_im = jnp.swapaxes(bb_im, 1, 2)[None]
    rr, ri_ = rev_re[:, :, None, :], rev_im[:, :, None, :]
    wbt = jnp.stack([halves(rr * bbt_re - ri_ * bbt_im),
                     halves(rr * bbt_im + ri_ * bbt_re)], axis=4)
    wst = wbt[:, :, :, :, :, None, :] * eye_h[None, None, :, None, None, :, None]
    wst = wst.reshape(2, S5_L, 128, 2 * S5_HALF_STATE)
    cpn = jnp.stack([halves(cp_re[1:S5_L + 1]), halves(-cp_im[1:S5_L + 1])], axis=4)
    wout = cpn[:, :, :, :, :, None, :] * eye_h[None, None, :, None, None, :, None]
    wout = wout.reshape(2, S5_L, 128, 2 * S5_HALF_STATE)
    lam_l = jnp.stack([pw_re[S5_L].reshape(2, S5_HALF_STATE),
                       pw_im[S5_L].reshape(2, S5_HALF_STATE)], axis=1)
    return kbd.astype(BF16), wst.astype(BF16), wout.astype(BF16), lam_l


def _s5(proj, kbd, wst, wout, lam_l, d_skip, w_glu, layer, batch, seq):
    nchunk = seq // S5_L
    return pl.pallas_call(
        _s5_kernel,
        grid=(batch,),
        in_specs=[
            pl.BlockSpec((seq, 128), lambda b: (b, COL_U // 128)),
            pl.BlockSpec((seq, 128), lambda b: (b, COL_U // 128 + 1)),
            _layer_spec((S5_L, D_BRANCH, D_BRANCH), layer),
            _layer_spec((2, S5_L, 128, 2 * S5_HALF_STATE), layer),
            _layer_spec((2, S5_L, 128, 2 * S5_HALF_STATE), layer),
            _layer_spec((2, 2, S5_HALF_STATE), layer),
            _layer_spec((1, D_BRANCH), layer),
            _layer_spec((D_BRANCH, D_BRANCH), layer),
        ],
        out_specs=pl.BlockSpec((seq, D_BRANCH), lambda b: (b, 0)),
        out_shape=jax.ShapeDtypeStruct((batch * seq, D_BRANCH), F32),
        scratch_shapes=[
            pltpu.VMEM((2, nchunk, 2 * S5_HALF_STATE), F32),
            pltpu.VMEM((2, nchunk, 2 * S5_HALF_STATE), F32),
            pltpu.VMEM((seq, 128), F32),
            pltpu.VMEM((seq, 128), F32),
        ],
        compiler_params=_cparams(),
        name="s5",
    )(proj, proj, kbd, wst, wout, lam_l, d_skip, w_glu)


def _attn_kernel(q_ref, k_ref, v_ref, gains_ref, bias_ref, onesbd_ref, hm_ref, o_ref,
                 kp_s, vp_s):
    seq = q_ref.shape[0]
    qg_ref = gains_ref.at[0:1, :]
    kg_ref = gains_ref.at[1:2, :]
    kp_s[0:ATT_PAD, :] = jnp.zeros((ATT_PAD, D_BRANCH), BF16)
    vp_s[0:ATT_PAD, :] = jnp.zeros((ATT_PAD, D_BRANCH), BF16)
    rt = 256

    def prep(i, carry):
        rows = pl.ds(pl.multiple_of(i * rt, rt), rt)
        k = k_ref[rows, :]
        ms = _dot_exact_rhs(k * k, onesbd_ref[...]) * (1.0 / ATT_HEADDIM)
        kn = k * lax.rsqrt(ms + EPS) * kg_ref[...]
        prow = pl.ds(pl.multiple_of(ATT_PAD + i * rt, 64), rt)
        kp_s[prow, :] = kn.astype(BF16)
        vp_s[prow, :] = v_ref[rows, :].astype(BF16)
        return carry

    lax.fori_loop(0, seq // rt, prep, 0)

    pidx = lax.broadcasted_iota(jnp.int32, (ATT_HEADS * CHUNK, ATT_BAND), 1)
    scale = ATT_HEADDIM ** -0.5

    def chunk(c, carry):
        r0 = pl.multiple_of(c * CHUNK, CHUNK)
        q = q_ref[pl.ds(r0, CHUNK), :]
        ms = _dot_exact_rhs(q * q, onesbd_ref[...]) * (1.0 / ATT_HEADDIM)
        qn = q * lax.rsqrt(ms + EPS) * (qg_ref[...] * scale)
        qs = jnp.concatenate([qn * hm_ref[h:h + 1, :] for h in range(ATT_HEADS)], axis=0).astype(BF16)
        kb = kp_s[pl.ds(r0, ATT_BAND), :]
        vb = vp_s[pl.ds(r0, ATT_BAND), :]
        s = _dot_nt(qs, kb)
        first_valid = jnp.maximum(ATT_LEFT_CHUNKS + 1 - c, 1) * CHUNK
        s = jnp.where(pidx >= first_valid, s + bias_ref[...], NEG_INF)
        m = jnp.max(s, axis=-1, keepdims=True)
        e = jnp.exp(s - m)
        l = jnp.sum(e, axis=-1, keepdims=True)
        pv = _dot(e.astype(BF16), vb) / l
        out = pv[0:CHUNK, :] * hm_ref[0:1, :]
        for h in range(1, ATT_HEADS):
            out = out + pv[h * CHUNK:(h + 1) * CHUNK, :] * hm_ref[h:h + 1, :]
        o_ref[pl.ds(r0, CHUNK), :] = out
        return carry

    lax.fori_loop(0, seq // CHUNK, chunk, 0, unroll=4)


def _attn_bias(rel_bias):
    ndiag = CHUNK + ATT_BAND
    m = np.arange(ndiag)
    rel = np.clip(m - ATT_BAND + ATT_LEFT_CHUNKS * CHUNK + CHUNK, -ATT_MAX_REL, ATT_MAX_REL) + ATT_MAX_REL
    diag = rel_bias[:, rel[::-1].copy()]
    bias = jnp.stack([diag[:, CHUNK - 1 - t:CHUNK - 1 - t + ATT_BAND] for t in range(CHUNK)], axis=1)
    return bias.reshape(ATT_HEADS * CHUNK, ATT_BAND)


def _attn(proj, gains, bias, layer, batch, seq):
    full = _const_spec
    colblk = lambda j: pl.BlockSpec((seq, D_BRANCH), lambda b: (b, COL_QKV // D_BRANCH + j))
    return pl.pallas_call(
        _attn_kernel,
        grid=(batch,),
        in_specs=[
            colblk(0), colblk(1), colblk(2),
            _layer_spec((2, D_BRANCH), layer),
            _layer_spec((ATT_HEADS * CHUNK, ATT_BAND), layer),
            full((D_BRANCH, D_BRANCH)), full((4, D_BRANCH)),
        ],
        out_specs=pl.BlockSpec((seq, D_BRANCH), lambda b: (b, 0)),
        out_shape=jax.ShapeDtypeStruct((batch * seq, D_BRANCH), F32),
        scratch_shapes=[
            pltpu.VMEM((ATT_PAD + seq, D_BRANCH), BF16),
            pltpu.VMEM((ATT_PAD + seq, D_BRANCH), BF16),
        ],
        compiler_params=_cparams(),
        name="attn",
    )(proj, proj, proj, gains, bias, _ones_bd(64), _head_masks())


def kernel(x, p, norm_mix, w_in, hg_lb_logits, hg_o_norm, ssd_conv_w, ssd_conv_b, ssd_dt_bias, ssd_A_log, ssd_D, ssd_norm, s5_A_re, s5_A_im, s5_B_re, s5_B_im, s5_C_re, s5_C_im, s5_D, s5_log_dt, s5_w_glu, att_q_norm, att_k_norm, att_rel_bias, w_branch, w_out, norm_ffn, w_ff1, w_ff2, w_ple, norm_ple, w_ple_gate):
    batch, seq, _ = x.shape
    depth = w_in.shape[0]
    t = batch * seq
    row3 = lambda a: a.reshape(depth, 1, -1)

    lb_all = jnp.cumsum(jax.nn.softmax(hg_lb_logits.astype(F32), axis=0), axis=0)
    lb_all = lb_all - lb_all[0:1]
    hg_vec = jnp.stack([jnp.log(lb_all), jnp.log1p(-lb_all), hg_o_norm], axis=1)
    pad4 = lambda v: jnp.pad(v, ((0, 0), (0, DT_PAD - SSD_HEADS)))
    ssd_v128 = jnp.stack([pad4(ssd_dt_bias), pad4(ssd_A_log)], axis=1)
    ssd_v256 = jnp.stack([jnp.repeat(ssd_D, SSD_HEADDIM, axis=1), ssd_norm], axis=1)
    conv_wt = jnp.swapaxes(ssd_conv_w, 1, 2)
    kbd, wst, wout, lam_l = jax.vmap(_s5_weights)(s5_A_re, s5_A_im, s5_B_re, s5_B_im, s5_C_re,
                                                  s5_C_im, s5_log_dt)
    att_gains = jnp.stack([jnp.tile(att_q_norm, (1, ATT_HEADS)), jnp.tile(att_k_norm, (1, ATT_HEADS))],
                          axis=1)
    att_bias = jax.vmap(_attn_bias)(att_rel_bias)
    w_mix, w_gate = _repack(w_in)
    w_branch16, w_out16, w_glu16 = w_branch.astype(BF16), w_out.astype(BF16), s5_w_glu.astype(BF16)
    w_ff1_16, w_ff2_16 = w_ff1.astype(BF16), w_ff2.astype(BF16)
    w_pg16, w_ple16 = w_ple_gate.astype(BF16), w_ple.astype(BF16)
    g_mix, g_ffn, g_ple = row3(norm_mix), row3(norm_ffn), row3(norm_ple)

    x2 = x.reshape(t, D_MODEL)
    p3 = p.reshape(depth, t, D_PLE)
    for i in range(depth):
        proj = _inproj(x2, g_mix, w_mix, i)
        y_a = _hgrn(proj, hg_vec, i, batch, seq)
        y_b = _ssd(proj, conv_wt, row3(ssd_conv_b), ssd_v128, ssd_v256, i, batch, seq)
        y_c = _s5(proj, kbd, wst, wout, lam_l, row3(s5_D), w_glu16, i, batch, seq)
        y_d = _attn(proj, att_gains, att_bias, i, batch, seq)
        x2 = _merge(x2, y_a, y_b, y_c, y_d, g_mix, w_gate, w_branch16, w_out16, i)
        x2 = _ffn_ple(x2, p3, g_ffn, w_ff1_16, w_ff2_16, g_ple, w_pg16, w_ple16, i)
    return x2.reshape(batch, seq, D_MODEL)
```

```python
import math

import jax
import jax.numpy as jnp
import numpy as np
from jax import lax
from jax.experimental import pallas as pl
from jax.experimental.pallas import tpu as pltpu

F32 = jnp.float32
BF16 = jnp.bfloat16

D_MODEL = 1024
CHUNK = 64
D_PLE = 256
N_BRANCH = 4
D_BRANCH = 256
D_FF = 4096
EPS = 1e-6
NEG_INF = -1e30
LOG2E = math.log2(math.e)

HG_HEADS = 4
HG_DK = 64
SSD_HEADS = 4
SSD_HEADDIM = 64
SSD_GROUPS = 2
SSD_DSTATE = 64
SSD_CONV = 4
SSD_INNER = 256
SSD_XBC = 512
S5_GROUP_CH = 16
S5_GROUPS = 16
S5_STATE = 64
ATT_HEADS = 4
ATT_HEADDIM = 64
ATT_LEFT_CHUNKS = 8
ATT_MAX_REL = 128

COL_HG = 0
COL_XBC = 1024
COL_Z = 1536
COL_U = 1792
COL_QKV = 2048
COL_DT = 2816
N_MIX = 3072
DT_PAD = 128

W_HG, W_Z, W_XBC, W_DT, W_U, W_QKV, W_GATE = 0, 1024, 1280, 1792, 1796, 2052, 2820
N_IN = 6916

S5_L = 8
S5_HALF_G = 8
S5_HALF_STATE = S5_HALF_G * S5_STATE

ATT_BAND = (ATT_LEFT_CHUNKS + 2) * CHUNK
ATT_PAD = (ATT_LEFT_CHUNKS + 1) * CHUNK

VMEM_LIMIT = 56 * 1024 * 1024


def _cparams(n_axes=1):
    return pltpu.CompilerParams(
        dimension_semantics=("arbitrary",) * n_axes, vmem_limit_bytes=VMEM_LIMIT)


def _layer_spec(shape, layer):
    return pl.BlockSpec((None,) + tuple(shape), lambda *_: (layer,) + (0,) * len(shape),
                        pipeline_mode=pl.Buffered(1))


def _const_spec(shape):
    return pl.BlockSpec(tuple(shape), lambda *_: (0,) * len(shape), pipeline_mode=pl.Buffered(1))


def _dot(a, b):
    return jnp.dot(a, b, preferred_element_type=F32)


def _dot_nt(a, b):
    return lax.dot_general(a, b, (((1,), (1,)), ((), ())), preferred_element_type=F32)


def _dot_tn(a, b):
    return lax.dot_general(a, b, (((0,), (0,)), ((), ())), preferred_element_type=F32)


def _split3(a):
    a1 = a.astype(BF16)
    r1 = a - a1.astype(F32)
    a2 = r1.astype(BF16)
    r2 = r1 - a2.astype(F32)
    return a1, a2, r2.astype(BF16)


def _dot_exact_rhs(a, m):
    a1, a2, a3 = _split3(a)
    return _dot(a1, m) + _dot(a2, m) + _dot(a3, m)


def _dot_exact_lhs(m, a):
    a1, a2, a3 = _split3(a)
    return _dot(m, a1) + _dot(m, a2) + _dot(m, a3)


def _sigmoid(x):
    return 1.0 / (1.0 + jnp.exp(-x))


def _softplus(x):
    return jnp.maximum(x, 0.0) + jnp.log1p(jnp.exp(-jnp.abs(x)))


def _rms_rows(x, g):
    return x * lax.rsqrt(jnp.mean(x * x, axis=-1, keepdims=True) + EPS) * g


def _repack_kernel(w_ref, e_ref, mix_ref, gate_ref):
    def piece(lo, hi):
        return w_ref[:, lo:hi].astype(BF16)

    mix_ref[:, COL_HG:COL_HG + 1024] = piece(W_HG, W_HG + 1024)
    mix_ref[:, COL_XBC:COL_XBC + SSD_XBC] = piece(W_XBC, W_XBC + SSD_XBC)
    mix_ref[:, COL_Z:COL_Z + SSD_INNER] = piece(W_Z, W_Z + SSD_INNER)
    mix_ref[:, COL_U:COL_U + D_BRANCH] = piece(W_U, W_U + D_BRANCH)
    mix_ref[:, COL_QKV:COL_QKV + 3 * D_BRANCH] = piece(W_QKV, W_QKV + 3 * D_BRANCH)
    dt = _dot(w_ref[:, W_DT:W_DT + DT_PAD].astype(BF16), e_ref[...])
    mix_ref[:, COL_DT:COL_DT + D_BRANCH] = dt.astype(BF16)
    gate_ref[...] = piece(W_GATE, N_IN)


def _repack(w_in, tk=128):
    depth = w_in.shape[0]
    return pl.pallas_call(
        _repack_kernel,
        grid=(depth, D_MODEL // tk),
        in_specs=[pl.BlockSpec((None, tk, N_IN), lambda l, r: (l, r, 0)),
                  _const_spec((DT_PAD, D_BRANCH))],
        out_specs=[pl.BlockSpec((None, tk, N_MIX), lambda l, r: (l, r, 0)),
                   pl.BlockSpec((None, tk, N_BRANCH * D_MODEL), lambda l, r: (l, r, 0))],
        out_shape=[jax.ShapeDtypeStruct((depth, D_MODEL, N_MIX), BF16),
                   jax.ShapeDtypeStruct((depth, D_MODEL, N_BRANCH * D_MODEL), BF16)],
        compiler_params=_cparams(2),
        name="repack",
    )(w_in, _head_spread())


def _inproj_kernel(x_ref, g_ref, w_ref, o_ref):
    h = _rms_rows(x_ref[...], g_ref[...])
    o_ref[...] = _dot(h.astype(BF16), w_ref[...])


def _inproj(x2, g, w_mix, layer, tm=512):
    t = x2.shape[0]
    return pl.pallas_call(
        _inproj_kernel,
        grid=(t // tm,),
        in_specs=[
            pl.BlockSpec((tm, D_MODEL), lambda i: (i, 0)),
            _layer_spec((1, D_MODEL), layer),
            _layer_spec((D_MODEL, N_MIX), layer),
        ],
        out_specs=pl.BlockSpec((tm, N_MIX), lambda i: (i, 0)),
        out_shape=jax.ShapeDtypeStruct((t, N_MIX), F32),
        compiler_params=_cparams(),
        name="inproj",
    )(x2, g, w_mix)


def _merge_kernel(x_ref, ya_ref, yb_ref, yc_ref, yd_ref, g_ref, wg_ref, wb_ref, wo_ref, o_ref):
    x = x_ref[...]
    h = _rms_rows(x, g_ref[...]).astype(BF16)
    merged = None
    for m, y_ref in enumerate((ya_ref, yb_ref, yc_ref, yd_ref)):
        gate = _sigmoid(_dot(h, wg_ref[:, m * D_MODEL:(m + 1) * D_MODEL]))
        term = gate * _dot(y_ref[...].astype(BF16), wb_ref[m])
        merged = term if merged is None else merged + term
    o_ref[...] = x + _dot(merged.astype(BF16), wo_ref[...])


def _merge(x2, ya, yb, yc, yd, g, w_gate, w_branch, w_out, layer, tm=512):
    t = x2.shape[0]
    row = lambda i: (i, 0)
    return pl.pallas_call(
        _merge_kernel,
        grid=(t // tm,),
        in_specs=[
            pl.BlockSpec((tm, D_MODEL), row),
            pl.BlockSpec((tm, D_BRANCH), row),
            pl.BlockSpec((tm, D_BRANCH), row),
            pl.BlockSpec((tm, D_BRANCH), row),
            pl.BlockSpec((tm, D_BRANCH), row),
            _layer_spec((1, D_MODEL), layer),
            _layer_spec((D_MODEL, N_BRANCH * D_MODEL), layer),
            _layer_spec((N_BRANCH, D_BRANCH, D_MODEL), layer),
            _layer_spec((D_MODEL, D_MODEL), layer),
        ],
        out_specs=pl.BlockSpec((tm, D_MODEL), row),
        out_shape=jax.ShapeDtypeStruct((t, D_MODEL), F32),
        compiler_params=_cparams(),
        name="merge",
    )(x2, ya, yb, yc, yd, g, w_gate, w_branch, w_out)


def _ffn_ple_kernel(x_ref, p_ref, g1_ref, w1_ref, w2_ref, g2_ref, wg_ref, wp_ref, o_ref):
    x = x_ref[...]
    h = _rms_rows(x, g1_ref[...]).astype(BF16)
    acc = x
    tk = 1024
    for j in range(D_FF // tk):
        a = jnp.maximum(_dot(h, w1_ref[:, j * tk:(j + 1) * tk]), 0.0)
        acc = acc + _dot((a * a).astype(BF16), w2_ref[j * tk:(j + 1) * tk, :])
    h2 = _rms_rows(acc, g2_ref[...]).astype(BF16)
    gate = _sigmoid(_dot(h2, wg_ref[...]))
    o_ref[...] = acc + _dot(p_ref[...].astype(BF16), wp_ref[...]) * gate


def _ffn_ple(x2, p3, g_ffn, w1, w2, g_ple, w_gate, w_ple, layer, tm=512):
    t = x2.shape[0]
    return pl.pallas_call(
        _ffn_ple_kernel,
        grid=(t // tm,),
        in_specs=[
            pl.BlockSpec((tm, D_MODEL), lambda i: (i, 0)),
            pl.BlockSpec((None, tm, D_PLE), lambda i: (layer, i, 0)),
            _layer_spec((1, D_MODEL), layer),
            _layer_spec((D_MODEL, D_FF), layer),
            _layer_spec((D_FF, D_MODEL), layer),
            _layer_spec((1, D_MODEL), layer),
            _layer_spec((D_MODEL, D_MODEL), layer),
            _layer_spec((D_PLE, D_MODEL), layer),
        ],
        out_specs=pl.BlockSpec((tm, D_MODEL), lambda i: (i, 0)),
        out_shape=jax.ShapeDtypeStruct((t, D_MODEL), F32),
        compiler_params=_cparams(),
        name="ffn_ple",
    )(x2, p3, g_ffn, w1, w2, g_ple, w_gate, w_ple)


def _tri64():
    i = np.arange(CHUNK)
    return jnp.asarray(i[:, None] >= i[None, :], BF16)


def _ones_bd(block, n=D_BRANCH):
    i = np.arange(n) // block
    return jnp.asarray(i[:, None] == i[None, :], BF16)


def _head_spread():
    e = np.zeros((DT_PAD, D_BRANCH), np.float32)
    for h in range(SSD_HEADS):
        e[h, h * 64:(h + 1) * 64] = 1.0
    return jnp.asarray(e, BF16)


def _head_masks():
    lane = np.arange(D_BRANCH) // 64
    return jnp.asarray(lane[None, :] == np.arange(4)[:, None], F32)


HG_SUB = 16
HG_TILE = 256
HG_SUB_STACK = 8 * HG_SUB + 4 * HG_SUB


def _hgrn_kernel(p_ref, vec_ref, tri_ref, onesbd_ref, bdmask_ref, o_ref, st_ref, bl_s, ck_s, w_s):
    seq = p_ref.shape[0]
    half = HG_SUB // 2
    loglb = vec_ref[0:1, :]
    log1mlb = vec_ref[1:2, :]
    gain = vec_ref[2:3, :]

    def pre(i, carry):
        rows = pl.ds(pl.multiple_of(i * HG_TILE, HG_TILE), HG_TILE)
        z = p_ref[rows, 256:512]
        ls = jnp.minimum(z, 0.0) - jnp.log1p(jnp.exp(-jnp.abs(z)))
        y = log1mlb + ls
        lf = jnp.maximum(loglb, y) + jnp.log1p(jnp.exp(-jnp.abs(loglb - y)))
        bl2 = _dot_exact_lhs(tri_ref[...], lf) * LOG2E
        bl_s[rows, :] = bl2
        ck_s[rows, :] = bl2 - (log1mlb + (ls - z)) * LOG2E
        return carry

    lax.fori_loop(0, seq // HG_TILE, pre, 0)

    row16 = lax.broadcasted_iota(jnp.int32, (HG_SUB, D_BRANCH), 0)
    row8 = lax.broadcasted_iota(jnp.int32, (half, D_BRANCH), 0)

    def chunk(c, carry):
        r0 = c * CHUNK
        subs = []
        for s in range(CHUNK // HG_SUB):
            rs = pl.multiple_of(r0 + s * HG_SUB, HG_SUB)
            rows = pl.ds(rs, HG_SUB)
            q = p_ref[rows, 0:256]
            bl = bl_s[rows, :]
            base = s * HG_SUB_STACK
            for j in range(half):
                d = jnp.where(row16 >= j, bl - ck_s[pl.ds(rs + j, 1), :], NEG_INF)
                w_s[base + j * HG_SUB:base + (j + 1) * HG_SUB, :] = (q * jnp.exp2(d)).astype(BF16)
            q_hi = q[half:, :]
            bl_hi = bl[half:, :]
            for jp in range(half // 2):
                pieces = []
                for j in (half + 2 * jp, half + 2 * jp + 1):
                    d = jnp.where(row8 >= j - half, bl_hi - ck_s[pl.ds(rs + j, 1), :], NEG_INF)
                    pieces.append(q_hi * jnp.exp2(d))
                off = base + half * HG_SUB + jp * HG_SUB
                w_s[off:off + HG_SUB, :] = jnp.concatenate(pieces, axis=0).astype(BF16)
            subs.append((rs, rows, q, bl))

        r = _dot(w_s[...], onesbd_ref[...])

        for s, (rs, rows, q, bl) in enumerate(subs):
            base = s * HG_SUB_STACK
            o_lo = jnp.zeros((half, D_BRANCH), F32)
            o_hi = jnp.zeros((half, D_BRANCH), F32)
            for j in range(HG_SUB):
                vj = p_ref[pl.ds(rs + j, 1), 512:768]
                if j < half:
                    o_lo = o_lo + r[base + j * HG_SUB:base + j * HG_SUB + half, :] * vj
                    o_hi = o_hi + r[base + j * HG_SUB + half:base + (j + 1) * HG_SUB, :] * vj
                else:
                    off = base + half * HG_SUB + (j - half) * half
                    o_hi = o_hi + r[off:off + half, :] * vj
            o_ref[rows, :] = jnp.concatenate([o_lo, o_hi], axis=0)
        return carry

    lax.fori_loop(0, seq // CHUNK, chunk, 0, unroll=2)

    st_ref[...] = jnp.zeros_like(st_ref)
    nsub = CHUNK // HG_SUB

    def carry_state(c, carry):
        rows = pl.ds(pl.multiple_of(c * CHUNK, CHUNK), CHUNK)
        q = p_ref[rows, 0:256]
        v = p_ref[rows, 512:768].astype(BF16)
        b = bl_s[rows, :]
        ck = ck_s[rows, :]
        ends = [b[(s + 1) * HG_SUB - 1:(s + 1) * HG_SUB, :] for s in range(nsub)]
        st = st_ref[...]
        o = _dot_nt((q * jnp.exp2(b)).astype(BF16), st.astype(BF16))
        st = st * jnp.exp2(ends[-1])
        for s in range(nsub):
            sub = slice(s * HG_SUB, (s + 1) * HG_SUB)
            ka = jnp.exp2(ends[s] - ck[sub, :]).astype(BF16)
            upd = _dot_tn(v[sub, :], ka) * bdmask_ref[...]
            if s + 1 < nsub:
                after = slice((s + 1) * HG_SUB, CHUNK)
                lhs = (q[after, :] * jnp.exp2(b[after, :] - ends[s])).astype(BF16)
                o = jnp.concatenate([o[:(s + 1) * HG_SUB, :],
                                     o[after, :] + _dot_nt(lhs, upd.astype(BF16))], axis=0)
                st = st + upd * jnp.exp2(ends[-1] - ends[s])
            else:
                st = st + upd
        o_ref[rows, :] = o_ref[rows, :] + o
        st_ref[...] = st
        return carry

    lax.fori_loop(0, seq // CHUNK, carry_state, 0, unroll=2)

    def post(i, carry):
        rows = pl.ds(pl.multiple_of(i * HG_TILE, HG_TILE), HG_TILE)
        o = o_ref[rows, :]
        g = p_ref[rows, 768:1024]
        ms = _dot_exact_rhs(o * o, onesbd_ref[...]) * (1.0 / HG_DK)
        o_ref[rows, :] = o * lax.rsqrt(ms + EPS) * gain * (g * _sigmoid(g))
        return carry

    lax.fori_loop(0, seq // HG_TILE, post, 0)


def _hgrn(proj, hg_vec, layer, batch, seq):
    bd = _ones_bd(64)
    i = np.arange(HG_TILE)
    tri = jnp.asarray((i[:, None] // CHUNK == i[None, :] // CHUNK) & (i[:, None] >= i[None, :]), BF16)
    return pl.pallas_call(
        _hgrn_kernel,
        grid=(batch,),
        in_specs=[
            pl.BlockSpec((seq, 1024), lambda b: (b, COL_HG // 1024)),
            _layer_spec((3, D_BRANCH), layer),
            _const_spec((HG_TILE, HG_TILE)), _const_spec((D_BRANCH, D_BRANCH)),
            _const_spec((D_BRANCH, D_BRANCH)),
        ],
        out_specs=pl.BlockSpec((seq, D_BRANCH), lambda b: (b, 0)),
        out_shape=jax.ShapeDtypeStruct((batch * seq, D_BRANCH), F32),
        scratch_shapes=[
            pltpu.VMEM((D_BRANCH, D_BRANCH), F32),
            pltpu.VMEM((seq, D_BRANCH), F32),
            pltpu.VMEM((seq, D_BRANCH), F32),
            pltpu.VMEM((CHUNK // HG_SUB * HG_SUB_STACK, D_BRANCH), BF16),
        ],
        compiler_params=_cparams(),
        name="hgrn",
    )(proj, hg_vec, tri, bd, bd.astype(F32))


def _ssd_kernel(xbc_ref, z_ref, dt_ref, cw_ref, cb_ref, v256_ref,
                tri_ref, u_ref, gm_ref, hm_ref, o_ref, sn_ref):
    seq = xbc_ref.shape[0]
    dx_ref = v256_ref.at[0:1, :]
    ng_ref = v256_ref.at[1:2, :]
    dtb_ref = v256_ref.at[2:3, :]
    alog_ref = v256_ref.at[3:4, :]
    sn_ref[...] = jnp.zeros_like(sn_ref)
    lidx = lax.broadcasted_iota(jnp.int32, (CHUNK, D_BRANCH), 0)
    sidx = lax.broadcasted_iota(jnp.int32, (CHUNK, D_BRANCH), 1) % CHUNK
    causal = lidx >= sidx
    lane128 = lax.broadcasted_iota(jnp.int32, (1, 2 * SSD_DSTATE), 1)
    a_neg = -jnp.exp(alog_ref[...])

    def chunk(c, carry):
        r0 = pl.multiple_of(c * CHUNK, CHUNK)
        rows = pl.ds(r0, CHUNK)
        cur = xbc_ref[rows, :]
        prev = xbc_ref[pl.ds(pl.multiple_of(jnp.maximum(r0 - 8, 0), 8), 8), :]
        prev = prev * jnp.where(c > 0, 1.0, 0.0)
        ext = jnp.concatenate([prev, cur], axis=0)
        conv = cb_ref[...] + cur * cw_ref[SSD_CONV - 1:SSD_CONV, :]
        for s in range(1, SSD_CONV):
            sh = pltpu.roll(ext, s, axis=0)[8:8 + CHUNK, :]
            conv = conv + sh * cw_ref[SSD_CONV - 1 - s:SSD_CONV - s, :]
        xa = conv * _sigmoid(conv)
        xs = xa[:, 0:SSD_INNER]
        bm = xa[:, SSD_INNER:SSD_INNER + 128]
        cm = xa[:, SSD_INNER + 128:SSD_INNER + 256]

        dtx = _softplus(dt_ref[rows, :] + dtb_ref[...])
        adtx = dtx * a_neg
        cum = _dot_exact_lhs(tri_ref[...], jnp.concatenate([adtx, adtx * u_ref[...]], axis=1))
        acum = cum[:, 0:D_BRANCH]
        seg = cum[:, D_BRANCH:2 * D_BRANCH]
        lmat = jnp.exp(jnp.where(causal, seg, NEG_INF))
        xdt = xs * dtx

        bm16 = bm.astype(BF16)
        y = jnp.zeros((CHUNK, D_BRANCH), F32)
        for grp in range(SSD_GROUPS):
            gsel = (lane128 // SSD_DSTATE) == grp
            gmat = _dot_nt(jnp.where(gsel, cm, 0.0).astype(BF16), bm16)
            for h in range(grp * 2, grp * 2 + 2):
                mh = gmat * lmat[:, h * CHUNK:(h + 1) * CHUNK]
                y = y + _dot(mh.astype(BF16), (xdt * hm_ref[h:h + 1, :]).astype(BF16))

        sn_old = sn_ref[...]
        y = y + _dot(cm.astype(BF16), sn_old.astype(BF16)) * jnp.exp(acum)
        aend = acum[CHUNK - 1:CHUNK, :]
        upd = _dot_tn(bm16, (jnp.exp(aend - acum) * xdt).astype(BF16))
        sn_ref[...] = jnp.exp(aend) * sn_old + upd * gm_ref[...]

        y = y + dx_ref[...] * xs
        zz = z_ref[rows, :]
        y = y * (zz * _sigmoid(zz))
        halves = []
        for grp in range(SSD_GROUPS):
            yh = y[:, grp * 128:(grp + 1) * 128]
            halves.append(yh * lax.rsqrt(jnp.mean(yh * yh, axis=-1, keepdims=True) + EPS))
        o_ref[rows, :] = jnp.concatenate(halves, axis=1) * ng_ref[...]
        return carry

    lax.fori_loop(0, seq // CHUNK, chunk, 0, unroll=8)


def _ssd(proj, conv_wt, conv_b, v256, layer, batch, seq):
    full = _const_spec
    li = np.arange(CHUNK)
    u_t = np.tile((li[:, None] > li[None, :]).astype(np.float32), (1, SSD_HEADS))
    gm = ((np.arange(128) // 64)[:, None] == (np.arange(256) // 128)[None, :]).astype(np.float32)
    return pl.pallas_call(
        _ssd_kernel,
        grid=(batch,),
        in_specs=[
            pl.BlockSpec((seq, SSD_XBC), lambda b: (b, COL_XBC // SSD_XBC)),
            pl.BlockSpec((seq, SSD_INNER), lambda b: (b, COL_Z // SSD_INNER)),
            pl.BlockSpec((seq, D_BRANCH), lambda b: (b, COL_DT // D_BRANCH)),
            _layer_spec((SSD_CONV, SSD_XBC), layer), _layer_spec((1, SSD_XBC), layer),
            _layer_spec((4, D_BRANCH), layer),
            full((CHUNK, CHUNK)), full((CHUNK, D_BRANCH)),
            full((128, D_BRANCH)), full((4, D_BRANCH)),
        ],
        out_specs=pl.BlockSpec((seq, D_BRANCH), lambda b: (b, 0)),
        out_shape=jax.ShapeDtypeStruct((batch * seq, D_BRANCH), F32),
        scratch_shapes=[pltpu.VMEM((2 * SSD_DSTATE, D_BRANCH), F32)],
        compiler_params=_cparams(),
        name="ssd",
    )(proj, proj, proj, conv_wt, conv_b, v256,
      _tri64(), jnp.asarray(u_t), jnp.asarray(gm), _head_masks())


S5_RT = 256


def _gelu_tanh(x):
    return 0.5 * x * (1.0 + jnp.tanh(math.sqrt(2.0 / math.pi) * (x + 0.044715 * (x * x * x))))


def _s5_kernel(ua_ref, ub_ref, kbd_ref, wst_ref, wout_ref, lam_ref, d_ref, wglu_ref, o_ref,
               hloc_s, hprev_s, ya_s, yb_s):
    u_refs = (ua_ref, ub_ref)
    y_refs = (ya_s, yb_s)
    seq = ua_ref.shape[0]
    nchunk = seq // S5_L
    hs = S5_HALF_STATE
    rowmod = lax.broadcasted_iota(jnp.int32, (S5_RT, D_BRANCH), 0) % S5_L

    def tile(i, carry):
        rows = pl.ds(pl.multiple_of(i * S5_RT, S5_RT), S5_RT)
        ut = jnp.concatenate([ua_ref[rows, :], ub_ref[rows, :]], axis=1)
        acc = _dot(ut.astype(BF16), kbd_ref[0])
        for d in range(1, S5_L):
            ud = jnp.where(rowmod >= d, pltpu.roll(ut, d, axis=0), 0.0)
            acc = acc + _dot(ud.astype(BF16), kbd_ref[d])
        ya_s[rows, :] = acc[:, 0:128]
        yb_s[rows, :] = acc[:, 128:256]
        return carry

    lax.fori_loop(0, seq // S5_RT, tile, 0)

    for half in range(2):
        acc = None
        for s in range(S5_L):
            us = u_refs[half][pl.ds(s, nchunk, stride=S5_L), :]
            term = _dot(us.astype(BF16), wst_ref[half, s])
            acc = term if acc is None else acc + term
        hloc_s[half] = acc

    for half in range(2):
        lr = lam_ref[half, 0:1, :]
        li = lam_ref[half, 1:2, :]

        def step(c, carry):
            hr, hi = carry
            hprev_s[half, pl.ds(c, 1), 0:hs] = hr
            hprev_s[half, pl.ds(c, 1), hs:2 * hs] = hi
            loc = hloc_s[half, pl.ds(c, 1), :]
            return (lr * hr - li * hi + loc[:, 0:hs], lr * hi + li * hr + loc[:, hs:2 * hs])

        zero = jnp.zeros((1, hs), F32)
        lax.fori_loop(0, nchunk, step, (zero, zero))

    for half in range(2):
        hp = hprev_s[half].astype(BF16)
        for s in range(S5_L):
            srows = pl.ds(s, nchunk, stride=S5_L)
            y_refs[half][srows, :] = y_refs[half][srows, :] + _dot_nt(hp, wout_ref[half, s])

    def tail(i, carry):
        rows = pl.ds(pl.multiple_of(i * S5_RT, S5_RT), S5_RT)
        ut = jnp.concatenate([ua_ref[rows, :], ub_ref[rows, :]], axis=1)
        yt = jnp.concatenate([ya_s[rows, :], yb_s[rows, :]], axis=1)
        y = _gelu_tanh(yt + d_ref[...] * ut)
        o_ref[rows, :] = y * _sigmoid(_dot(y.astype(BF16), wglu_ref[...]))
        return carry

    lax.fori_loop(0, seq // S5_RT, tail, 0)


def _s5_weights(a_re, a_im, b_re, b_im, c_re, c_im, log_dt):
    hp = lax.Precision.HIGHEST
    step = jnp.exp(log_dt)[:, None]
    mag = jnp.exp(a_re * step)
    lam_re = mag * jnp.cos(a_im * step)
    lam_im = mag * jnp.sin(a_im * step)
    den = a_re * a_re + a_im * a_im
    num_re = lam_re - 1.0
    coef_re = (num_re * a_re + lam_im * a_im) / den
    coef_im = (lam_im * a_re - num_re * a_im) / den
    bb_re = coef_re[..., None] * b_re - coef_im[..., None] * b_im
    bb_im = coef_re[..., None] * b_im + coef_im[..., None] * b_re

    pw_re = [jnp.ones_like(lam_re)]
    pw_im = [jnp.zeros_like(lam_im)]
    for _ in range(S5_L):
        pr, pi = pw_re[-1], pw_im[-1]
        pw_re.append(pr * lam_re - pi * lam_im)
        pw_im.append(pr * lam_im + pi * lam_re)
    rev_re = jnp.stack(pw_re[S5_L - 1::-1])
    rev_im = jnp.stack(pw_im[S5_L - 1::-1])
    pw_re = jnp.stack(pw_re)
    pw_im = jnp.stack(pw_im)

    cp_re = c_re[None] * pw_re[:, :, None, :] - c_im[None] * pw_im[:, :, None, :]
    cp_im = c_re[None] * pw_im[:, :, None, :] + c_im[None] * pw_re[:, :, None, :]
    kd = (jnp.einsum('dgip,gpj->dgij', cp_re[:S5_L], bb_re, precision=hp)
          - jnp.einsum('dgip,gpj->dgij', cp_im[:S5_L], bb_im, precision=hp))
    eye_g = jnp.eye(S5_GROUPS, dtype=F32)
    kbd = (jnp.transpose(kd, (0, 1, 3, 2))[:, :, :, None, :] * eye_g[None, :, None, :, None])
    kbd = kbd.reshape(S5_L, D_BRANCH, D_BRANCH)

    def halves(t):
        return jnp.moveaxis(t.reshape((S5_L, 2, S5_HALF_G) + t.shape[2:]), 1, 0)

    bbt_re = jnp.swapaxes(bb_re, 1, 2)[None]
    bbt_im = jnp.swapaxes(bb_im, 1, 2)[None]
    rr, ri_ = rev_re[:, :, None, :], rev_im[:, :, None, :]
    wbt = jnp.stack([halves(rr * bbt_re - ri_ * bbt_im),
                     halves(rr * bbt_im + ri_ * bbt_re)], axis=4)
    cpn = jnp.stack([halves(cp_re[1:S5_L + 1]), halves(-cp_im[1:S5_L + 1])], axis=4)
    compact = jnp.stack([wbt, cpn]).reshape(2, 2, S5_L, S5_HALF_G, S5_GROUP_CH, 2 * S5_STATE)
    lam_l = jnp.stack([pw_re[S5_L].reshape(2, S5_HALF_STATE),
                       pw_im[S5_L].reshape(2, S5_HALF_STATE)], axis=1)
    return kbd.astype(BF16), compact, lam_l


def _s5_expand_kernel(c_ref, o_ref):
    o_ref[...] = jnp.zeros(o_ref.shape, BF16)
    for s in range(S5_L):
        for g in range(S5_HALF_G):
            blk = c_ref[s, g]
            for ri in range(2):
                lo = ri * S5_HALF_STATE + g * S5_STATE
                o_ref[s, g * S5_GROUP_CH:(g + 1) * S5_GROUP_CH, lo:lo + S5_STATE] = (
                    blk[:, ri * S5_STATE:(ri + 1) * S5_STATE].astype(BF16))


def _s5_expand(compact):
    depth = compact.shape[0]
    blk_in = (None, None, None, S5_L, S5_HALF_G, S5_GROUP_CH, 2 * S5_STATE)
    blk_out = (None, None, None, S5_L, S5_HALF_G * S5_GROUP_CH, 2 * S5_HALF_STATE)
    idx = lambda l, w, h: (l, w, h, 0, 0, 0, 0)
    return pl.pallas_call(
        _s5_expand_kernel,
        grid=(depth, 2, 2),
        in_specs=[pl.BlockSpec(blk_in, idx)],
        out_specs=pl.BlockSpec(blk_out, lambda l, w, h: (l, w, h, 0, 0, 0)),
        out_shape=jax.ShapeDtypeStruct((depth, 2, 2, S5_L, 128, 2 * S5_HALF_STATE), BF16),
        compiler_params=_cparams(3),
        name="s5_expand",
    )(compact)


def _s5(proj, kbd, wmats, lam_l, d_skip, w_glu, layer, batch, seq):
    nchunk = seq // S5_L
    wspec = lambda which: pl.BlockSpec(
        (None, None, 2, S5_L, 128, 2 * S5_HALF_STATE), lambda b: (layer, which, 0, 0, 0, 0),
        pipeline_mode=pl.Buffered(1))
    return pl.pallas_call(
        _s5_kernel,
        grid=(batch,),
        in_specs=[
            pl.BlockSpec((seq, 128), lambda b: (b, COL_U // 128)),
            pl.BlockSpec((seq, 128), lambda b: (b, COL_U // 128 + 1)),
            _layer_spec((S5_L, D_BRANCH, D_BRANCH), layer),
            wspec(0), wspec(1),
            _layer_spec((2, 2, S5_HALF_STATE), layer),
            _layer_spec((1, D_BRANCH), layer),
            _layer_spec((D_BRANCH, D_BRANCH), layer),
        ],
        out_specs=pl.BlockSpec((seq, D_BRANCH), lambda b: (b, 0)),
        out_shape=jax.ShapeDtypeStruct((batch * seq, D_BRANCH), F32),
        scratch_shapes=[
            pltpu.VMEM((2, nchunk, 2 * S5_HALF_STATE), F32),
            pltpu.VMEM((2, nchunk, 2 * S5_HALF_STATE), F32),
            pltpu.VMEM((seq, 128), F32),
            pltpu.VMEM((seq, 128), F32),
        ],
        compiler_params=_cparams(),
        name="s5",
    )(proj, proj, kbd, wmats, wmats, lam_l, d_skip, w_glu)


def _attn_kernel(q_ref, k_ref, v_ref, gains_ref, bias_ref, onesbd_ref, hm_ref, o_ref,
                 kp_s, vp_s):
    seq = q_ref.shape[0]
    qg_ref = gains_ref.at[0:1, :]
    kg_ref = gains_ref.at[1:2, :]
    kp_s[0:ATT_PAD, :] = jnp.zeros((ATT_PAD, D_BRANCH), BF16)
    vp_s[0:ATT_PAD, :] = jnp.zeros((ATT_PAD, D_BRANCH), BF16)
    rt = 256

    def prep(i, carry):
        rows = pl.ds(pl.multiple_of(i * rt, rt), rt)
        k = k_ref[rows, :]
        ms = _dot_exact_rhs(k * k, onesbd_ref[...]) * (1.0 / ATT_HEADDIM)
        kn = k * lax.rsqrt(ms + EPS) * kg_ref[...]
        prow = pl.ds(pl.multiple_of(ATT_PAD + i * rt, 64), rt)
        kp_s[prow, :] = kn.astype(BF16)
        vp_s[prow, :] = v_ref[rows, :].astype(BF16)
        return carry

    lax.fori_loop(0, seq // rt, prep, 0)

    pidx = lax.broadcasted_iota(jnp.int32, (ATT_HEADS * CHUNK, ATT_BAND), 1)
    scale = ATT_HEADDIM ** -0.5

    def chunk(c, carry):
        r0 = pl.multiple_of(c * CHUNK, CHUNK)
        q = q_ref[pl.ds(r0, CHUNK), :]
        ms = _dot_exact_rhs(q * q, onesbd_ref[...]) * (1.0 / ATT_HEADDIM)
        qn = q * lax.rsqrt(ms + EPS) * (qg_ref[...] * scale)
        qs = jnp.concatenate([qn * hm_ref[h:h + 1, :] for h in range(ATT_HEADS)], axis=0).astype(BF16)
        kb = kp_s[pl.ds(r0, ATT_BAND), :]
        vb = vp_s[pl.ds(r0, ATT_BAND), :]
        s = _dot_nt(qs, kb)
        first_valid = jnp.maximum(ATT_LEFT_CHUNKS + 1 - c, 1) * CHUNK
        s = jnp.where(pidx >= first_valid, s + bias_ref[...], NEG_INF)
        m = jnp.max(s, axis=-1, keepdims=True)
        e = jnp.exp(s - m)
        l = jnp.sum(e, axis=-1, keepdims=True)
        pv = _dot(e.astype(BF16), vb) / l
        out = pv[0:CHUNK, :] * hm_ref[0:1, :]
        for h in range(1, ATT_HEADS):
            out = out + pv[h * CHUNK:(h + 1) * CHUNK, :] * hm_ref[h:h + 1, :]
        o_ref[pl.ds(r0, CHUNK), :] = out
        return carry

    lax.fori_loop(0, seq // CHUNK, chunk, 0, unroll=8)


def _attn_bias(rel_bias):
    n = CHUNK + ATT_BAND + 1
    j = np.arange(n)
    delta = np.where(j < ATT_BAND + 2, -j, n - j)
    rel = np.clip(delta + (ATT_LEFT_CHUNKS + 1) * CHUNK, -ATT_MAX_REL, ATT_MAX_REL) + ATT_MAX_REL
    d = rel_bias[:, rel]
    flat = jnp.tile(d, (1, CHUNK))[:, :CHUNK * (n - 1)]
    bias = flat.reshape(ATT_HEADS, CHUNK, n - 1)[:, :, :ATT_BAND]
    return bias.reshape(ATT_HEADS * CHUNK, ATT_BAND)


def _attn(proj, gains, bias, layer, batch, seq):
    full = _const_spec
    colblk = lambda j: pl.BlockSpec((seq, D_BRANCH), lambda b: (b, COL_QKV // D_BRANCH + j))
    return pl.pallas_call(
        _attn_kernel,
        grid=(batch,),
        in_specs=[
            colblk(0), colblk(1), colblk(2),
            _layer_spec((2, D_BRANCH), layer),
            _layer_spec((ATT_HEADS * CHUNK, ATT_BAND), layer),
            full((D_BRANCH, D_BRANCH)), full((4, D_BRANCH)),
        ],
        out_specs=pl.BlockSpec((seq, D_BRANCH), lambda b: (b, 0)),
        out_shape=jax.ShapeDtypeStruct((batch * seq, D_BRANCH), F32),
        scratch_shapes=[
            pltpu.VMEM((ATT_PAD + seq, D_BRANCH), BF16),
            pltpu.VMEM((ATT_PAD + seq, D_BRANCH), BF16),
        ],
        compiler_params=_cparams(),
        name="attn",
    )(proj, proj, proj, gains, bias, _ones_bd(64), _head_masks())


def kernel(x, p, norm_mix, w_in, hg_lb_logits, hg_o_norm, ssd_conv_w, ssd_conv_b, ssd_dt_bias, ssd_A_log, ssd_D, ssd_norm, s5_A_re, s5_A_im, s5_B_re, s5_B_im, s5_C_re, s5_C_im, s5_D, s5_log_dt, s5_w_glu, att_q_norm, att_k_norm, att_rel_bias, w_branch, w_out, norm_ffn, w_ff1, w_ff2, w_ple, norm_ple, w_ple_gate):
    batch, seq, _ = x.shape
    depth = w_in.shape[0]
    t = batch * seq
    row3 = lambda a: a.reshape(depth, 1, -1)

    lb_all = jnp.cumsum(jax.nn.softmax(hg_lb_logits.astype(F32), axis=0), axis=0)
    lb_all = lb_all - lb_all[0:1]
    hg_vec = jnp.stack([jnp.log(lb_all), jnp.log1p(-lb_all), hg_o_norm], axis=1)
    spread = lambda v: jnp.repeat(v, SSD_HEADDIM, axis=1)
    ssd_v256 = jnp.stack([spread(ssd_D), ssd_norm, spread(ssd_dt_bias), spread(ssd_A_log)], axis=1)
    conv_wt = jnp.swapaxes(ssd_conv_w, 1, 2)
    kbd, s5_compact, lam_l = jax.vmap(_s5_weights)(s5_A_re, s5_A_im, s5_B_re, s5_B_im, s5_C_re,
                                                   s5_C_im, s5_log_dt)
    s5_mats = _s5_expand(s5_compact)
    att_gains = jnp.stack([jnp.tile(att_q_norm, (1, ATT_HEADS)), jnp.tile(att_k_norm, (1, ATT_HEADS))],
                          axis=1)
    att_bias = jax.vmap(_attn_bias)(att_rel_bias)
    w_mix, w_gate = _repack(w_in)
    w_branch16, w_out16, w_glu16 = w_branch.astype(BF16), w_out.astype(BF16), s5_w_glu.astype(BF16)
    w_ff1_16, w_ff2_16 = w_ff1.astype(BF16), w_ff2.astype(BF16)
    w_pg16, w_ple16 = w_ple_gate.astype(BF16), w_ple.astype(BF16)
    g_mix, g_ffn, g_ple = row3(norm_mix), row3(norm_ffn), row3(norm_ple)

    x2 = x.reshape(t, D_MODEL)
    p3 = p.reshape(depth, t, D_PLE)
    for i in range(depth):
        proj = _inproj(x2, g_mix, w_mix, i)
        y_a = _hgrn(proj, hg_vec, i, batch, seq)
        y_b = _ssd(proj, conv_wt, row3(ssd_conv_b), ssd_v256, i, batch, seq)
        y_c = _s5(proj, kbd, s5_mats, lam_l, row3(s5_D), w_glu16, i, batch, seq)
        y_d = _attn(proj, att_gains, att_bias, i, batch, seq)
        x2 = _merge(x2, y_a, y_b, y_c, y_d, g_mix, w_gate, w_branch16, w_out16, i)
        x2 = _ffn_ple(x2, p3, g_ffn, w_ff1_16, w_ff2_16, g_ple, w_pg16, w_ple16, i)
    return x2.reshape(batch, seq, D_MODEL)
```

```python
import math

import jax
import jax.numpy as jnp
import numpy as np
from jax import lax
from jax.experimental import pallas as pl
from jax.experimental.pallas import tpu as pltpu

F32 = jnp.float32
BF16 = jnp.bfloat16

D_MODEL = 1024
CHUNK = 64
D_PLE = 256
N_BRANCH = 4
D_BRANCH = 256
D_FF = 4096
EPS = 1e-6
NEG_INF = -1e30
LOG2E = math.log2(math.e)

HG_HEADS = 4
HG_DK = 64
SSD_HEADS = 4
SSD_HEADDIM = 64
SSD_GROUPS = 2
SSD_DSTATE = 64
SSD_CONV = 4
SSD_INNER = 256
SSD_XBC = 512
S5_GROUP_CH = 16
S5_GROUPS = 16
S5_STATE = 64
ATT_HEADS = 4
ATT_HEADDIM = 64
ATT_LEFT_CHUNKS = 8
ATT_MAX_REL = 128

COL_HG = 0
COL_XBC = 1024
COL_Z = 1536
COL_U = 1792
COL_QKV = 2048
COL_DT = 2816
N_MIX = 3072
DT_PAD = 128

W_HG, W_Z, W_XBC, W_DT, W_U, W_QKV, W_GATE = 0, 1024, 1280, 1792, 1796, 2052, 2820
N_IN = 6916

S5_L = 8
S5_HALF_G = 8
S5_HALF_STATE = S5_HALF_G * S5_STATE

ATT_BAND = (ATT_LEFT_CHUNKS + 2) * CHUNK
ATT_PAD = (ATT_LEFT_CHUNKS + 1) * CHUNK

VMEM_LIMIT = 56 * 1024 * 1024


def _cparams(n_axes=1):
    return pltpu.CompilerParams(
        dimension_semantics=("arbitrary",) * n_axes, vmem_limit_bytes=VMEM_LIMIT)


def _layer_spec(shape, layer):
    return pl.BlockSpec((None,) + tuple(shape), lambda *_: (layer,) + (0,) * len(shape),
                        pipeline_mode=pl.Buffered(1))


def _const_spec(shape):
    return pl.BlockSpec(tuple(shape), lambda *_: (0,) * len(shape), pipeline_mode=pl.Buffered(1))


def _dot(a, b):
    return jnp.dot(a, b, preferred_element_type=F32)


def _dot_nt(a, b):
    return lax.dot_general(a, b, (((1,), (1,)), ((), ())), preferred_element_type=F32)


def _dot_tn(a, b):
    return lax.dot_general(a, b, (((0,), (0,)), ((), ())), preferred_element_type=F32)


def _split3(a):
    a1 = a.astype(BF16)
    r1 = a - a1.astype(F32)
    a2 = r1.astype(BF16)
    r2 = r1 - a2.astype(F32)
    return a1, a2, r2.astype(BF16)


def _dot_exact_rhs(a, m):
    a1, a2, a3 = _split3(a)
    return _dot(a1, m) + _dot(a2, m) + _dot(a3, m)


def _dot_exact_lhs(m, a):
    a1, a2, a3 = _split3(a)
    return _dot(m, a1) + _dot(m, a2) + _dot(m, a3)


def _sigmoid(x):
    return 1.0 / (1.0 + jnp.exp(-x))


def _softplus(x):
    return jnp.maximum(x, 0.0) + jnp.log1p(jnp.exp(-jnp.abs(x)))


def _rms_rows(x, g):
    return x * lax.rsqrt(jnp.mean(x * x, axis=-1, keepdims=True) + EPS) * g


REPACK_TN = 256
MIX_TILE_SRC = (W_HG, W_HG + 256, W_HG + 512, W_HG + 768, W_XBC, W_XBC + 256, W_Z, W_U,
                W_QKV, W_QKV + 256, W_QKV + 512, W_DT)
DT_TILE = COL_DT // REPACK_TN


def _repack_mix_kernel(src_ref, w_ref, sel_ref, o_ref):
    del src_ref
    j = pl.program_id(0)
    for l in range(o_ref.shape[0]):
        wt = w_ref[:, l, :].T.astype(BF16)

        @pl.when(j != DT_TILE)
        def _():
            o_ref[l] = wt

        @pl.when(j == DT_TILE)
        def _():
            o_ref[l] = _dot(wt, sel_ref[...]).astype(BF16)


def _repack_gate_kernel(w_ref, o_ref):
    for l in range(o_ref.shape[0]):
        o_ref[l] = w_ref[:, l, :].T.astype(BF16)


def _repack(w_in):
    depth = w_in.shape[0]
    w_t = jnp.transpose(w_in, (2, 0, 1))
    sel = np.zeros((REPACK_TN, D_BRANCH), np.float32)
    for h in range(SSD_HEADS):
        sel[h, h * 64:(h + 1) * 64] = 1.0
    w_mix = pl.pallas_call(
        _repack_mix_kernel,
        grid_spec=pltpu.PrefetchScalarGridSpec(
            num_scalar_prefetch=1,
            grid=(N_MIX // REPACK_TN,),
            in_specs=[pl.BlockSpec((pl.Element(REPACK_TN), pl.Element(depth), pl.Element(D_MODEL)),
                                   lambda j, src: (src[j], 0, 0)),
                      pl.BlockSpec((REPACK_TN, D_BRANCH), lambda j, src: (0, 0))],
            out_specs=pl.BlockSpec((depth, D_MODEL, REPACK_TN), lambda j, src: (0, 0, j)),
        ),
        out_shape=jax.ShapeDtypeStruct((depth, D_MODEL, N_MIX), BF16),
        compiler_params=_cparams(),
        name="repack_mix",
    )(jnp.asarray(MIX_TILE_SRC, jnp.int32), w_t, jnp.asarray(sel, BF16))
    n_gate = N_BRANCH * D_MODEL
    w_gate = pl.pallas_call(
        _repack_gate_kernel,
        grid=(n_gate // REPACK_TN,),
        in_specs=[pl.BlockSpec((pl.Element(REPACK_TN), pl.Element(depth), pl.Element(D_MODEL)),
                               lambda j: (W_GATE + j * REPACK_TN, 0, 0))],
        out_specs=pl.BlockSpec((depth, D_MODEL, REPACK_TN), lambda j: (0, 0, j)),
        out_shape=jax.ShapeDtypeStruct((depth, D_MODEL, n_gate), BF16),
        compiler_params=_cparams(),
        name="repack_gate",
    )(w_t)
    return w_mix, w_gate


def _inproj_kernel(x_ref, g_ref, w_ref, o_ref):
    h = _rms_rows(x_ref[...], g_ref[...])
    o_ref[...] = _dot(h.astype(BF16), w_ref[...])


def _inproj(x2, g, w_mix, layer, tm=512):
    t = x2.shape[0]
    return pl.pallas_call(
        _inproj_kernel,
        grid=(t // tm,),
        in_specs=[
            pl.BlockSpec((tm, D_MODEL), lambda i: (i, 0)),
            _layer_spec((1, D_MODEL), layer),
            _layer_spec((D_MODEL, N_MIX), layer),
        ],
        out_specs=pl.BlockSpec((tm, N_MIX), lambda i: (i, 0)),
        out_shape=jax.ShapeDtypeStruct((t, N_MIX), F32),
        compiler_params=_cparams(),
        name="inproj",
    )(x2, g, w_mix)


def _merge_kernel(x_ref, ya_ref, yb_ref, yc_ref, yd_ref, g_ref, wg_ref, wb_ref, wo_ref, o_ref):
    x = x_ref[...]
    h = _rms_rows(x, g_ref[...]).astype(BF16)
    merged = None
    for m, y_ref in enumerate((ya_ref, yb_ref, yc_ref, yd_ref)):
        gate = _sigmoid(_dot(h, wg_ref[:, m * D_MODEL:(m + 1) * D_MODEL]))
        term = gate * _dot(y_ref[...].astype(BF16), wb_ref[m])
        merged = term if merged is None else merged + term
    o_ref[...] = x + _dot(merged.astype(BF16), wo_ref[...])


def _merge(x2, ya, yb, yc, yd, g, w_gate, w_branch, w_out, layer, tm=512):
    t = x2.shape[0]
    row = lambda i: (i, 0)
    return pl.pallas_call(
        _merge_kernel,
        grid=(t // tm,),
        in_specs=[
            pl.BlockSpec((tm, D_MODEL), row),
            pl.BlockSpec((tm, D_BRANCH), row),
            pl.BlockSpec((tm, D_BRANCH), row),
            pl.BlockSpec((tm, D_BRANCH), row),
            pl.BlockSpec((tm, D_BRANCH), row),
            _layer_spec((1, D_MODEL), layer),
            _layer_spec((D_MODEL, N_BRANCH * D_MODEL), layer),
            _layer_spec((N_BRANCH, D_BRANCH, D_MODEL), layer),
            _layer_spec((D_MODEL, D_MODEL), layer),
        ],
        out_specs=pl.BlockSpec((tm, D_MODEL), row),
        out_shape=jax.ShapeDtypeStruct((t, D_MODEL), F32),
        compiler_params=_cparams(),
        name="merge",
    )(x2, ya, yb, yc, yd, g, w_gate, w_branch, w_out)


def _ffn_ple_kernel(x_ref, p_ref, g1_ref, w1_ref, w2_ref, g2_ref, wg_ref, wp_ref, o_ref):
    x = x_ref[...]
    h = _rms_rows(x, g1_ref[...]).astype(BF16)
    acc = x
    tk = 1024
    for j in range(D_FF // tk):
        a = jnp.maximum(_dot(h, w1_ref[:, j * tk:(j + 1) * tk]), 0.0)
        acc = acc + _dot((a * a).astype(BF16), w2_ref[j * tk:(j + 1) * tk, :])
    h2 = _rms_rows(acc, g2_ref[...]).astype(BF16)
    gate = _sigmoid(_dot(h2, wg_ref[...]))
    o_ref[...] = acc + _dot(p_ref[...].astype(BF16), wp_ref[...]) * gate


def _ffn_ple(x2, p3, g_ffn, w1, w2, g_ple, w_gate, w_ple, layer, tm=512):
    t = x2.shape[0]
    return pl.pallas_call(
        _ffn_ple_kernel,
        grid=(t // tm,),
        in_specs=[
            pl.BlockSpec((tm, D_MODEL), lambda i: (i, 0)),
            pl.BlockSpec((None, tm, D_PLE), lambda i: (layer, i, 0)),
            _layer_spec((1, D_MODEL), layer),
            _layer_spec((D_MODEL, D_FF), layer),
            _layer_spec((D_FF, D_MODEL), layer),
            _layer_spec((1, D_MODEL), layer),
            _layer_spec((D_MODEL, D_MODEL), layer),
            _layer_spec((D_PLE, D_MODEL), layer),
        ],
        out_specs=pl.BlockSpec((tm, D_MODEL), lambda i: (i, 0)),
        out_shape=jax.ShapeDtypeStruct((t, D_MODEL), F32),
        compiler_params=_cparams(),
        name="ffn_ple",
    )(x2, p3, g_ffn, w1, w2, g_ple, w_gate, w_ple)


def _tri64():
    i = np.arange(CHUNK)
    return jnp.asarray(i[:, None] >= i[None, :], BF16)


def _ones_bd(block, n=D_BRANCH):
    i = np.arange(n) // block
    return jnp.asarray(i[:, None] == i[None, :], BF16)


def _head_masks():
    lane = np.arange(D_BRANCH) // 64
    return jnp.asarray(lane[None, :] == np.arange(4)[:, None], F32)


HG_SUB = 16
HG_TILE = 256
HG_SUB_STACK = 8 * HG_SUB + 4 * HG_SUB


def _hgrn_kernel(p_ref, vec_ref, tri_ref, onesbd_ref, bdmask_ref, hm_ref, o_ref, st_ref, bl_s, ck_s, w_s):
    seq = p_ref.shape[0]
    half = HG_SUB // 2
    loglb = vec_ref[0:1, :]
    log1mlb = vec_ref[1:2, :]
    gain = vec_ref[2:3, :]

    def pre(i, carry):
        rows = pl.ds(pl.multiple_of(i * HG_TILE, HG_TILE), HG_TILE)
        z = p_ref[rows, 256:512]
        ls = jnp.minimum(z, 0.0) - jnp.log1p(jnp.exp(-jnp.abs(z)))
        y = log1mlb + ls
        lf = jnp.maximum(loglb, y) + jnp.log1p(jnp.exp(-jnp.abs(loglb - y)))
        bl2 = _dot_exact_lhs(tri_ref[...], lf) * LOG2E
        bl_s[rows, :] = bl2
        ck_s[rows, :] = bl2 - (log1mlb + (ls - z)) * LOG2E
        return carry

    lax.fori_loop(0, seq // HG_TILE, pre, 0)

    row16 = lax.broadcasted_iota(jnp.int32, (HG_SUB, D_BRANCH), 0)
    row8 = lax.broadcasted_iota(jnp.int32, (half, D_BRANCH), 0)

    def chunk(c, carry):
        r0 = c * CHUNK
        subs = []
        for s in range(CHUNK // HG_SUB):
            rs = pl.multiple_of(r0 + s * HG_SUB, HG_SUB)
            rows = pl.ds(rs, HG_SUB)
            q = p_ref[rows, 0:256]
            bl = bl_s[rows, :]
            base = s * HG_SUB_STACK
            for j in range(half):
                d = jnp.where(row16 >= j, bl - ck_s[pl.ds(rs + j, 1), :], NEG_INF)
                w_s[base + j * HG_SUB:base + (j + 1) * HG_SUB, :] = (q * jnp.exp2(d)).astype(BF16)
            q_hi = q[half:, :]
            bl_hi = bl[half:, :]
            for jp in range(half // 2):
                pieces = []
                for j in (half + 2 * jp, half + 2 * jp + 1):
                    d = jnp.where(row8 >= j - half, bl_hi - ck_s[pl.ds(rs + j, 1), :], NEG_INF)
                    pieces.append(q_hi * jnp.exp2(d))
                off = base + half * HG_SUB + jp * HG_SUB
                w_s[off:off + HG_SUB, :] = jnp.concatenate(pieces, axis=0).astype(BF16)
            subs.append((rs, rows, q, bl))

        r = _dot(w_s[...], onesbd_ref[...])

        for s, (rs, rows, q, bl) in enumerate(subs):
            base = s * HG_SUB_STACK
            o_lo = jnp.zeros((half, D_BRANCH), F32)
            o_hi = jnp.zeros((half, D_BRANCH), F32)
            for j in range(HG_SUB):
                vj = p_ref[pl.ds(rs + j, 1), 512:768]
                if j < half:
                    o_lo = o_lo + r[base + j * HG_SUB:base + j * HG_SUB + half, :] * vj
                    o_hi = o_hi + r[base + j * HG_SUB + half:base + (j + 1) * HG_SUB, :] * vj
                else:
                    off = base + half * HG_SUB + (j - half) * half
                    o_hi = o_hi + r[off:off + half, :] * vj
            o_ref[rows, :] = jnp.concatenate([o_lo, o_hi], axis=0)
        return carry

    lax.fori_loop(0, seq // CHUNK, chunk, 0, unroll=2)

    st_ref[...] = jnp.zeros_like(st_ref)
    nsub = CHUNK // HG_SUB

    def carry_state(c, carry):
        rows = pl.ds(pl.multiple_of(c * CHUNK, CHUNK), CHUNK)
        q = p_ref[rows, 0:256]
        v = p_ref[rows, 512:768].astype(BF16)
        b = bl_s[rows, :]
        ck = ck_s[rows, :]
        ends = [b[(s + 1) * HG_SUB - 1:(s + 1) * HG_SUB, :] for s in range(nsub)]
        st = st_ref[...]
        o = _dot_nt((q * jnp.exp2(b)).astype(BF16), st.astype(BF16))
        k_end = jnp.exp2(ends[-1] - ck).astype(BF16)
        st_ref[...] = st * jnp.exp2(ends[-1]) + _dot_tn(v, k_end) * bdmask_ref[...]
        for s in range(nsub - 1):
            sub = slice(s * HG_SUB, (s + 1) * HG_SUB)
            after = slice((s + 1) * HG_SUB, CHUNK)
            n = CHUNK - (s + 1) * HG_SUB
            qd = q[after, :] * jnp.exp2(b[after, :] - ends[s])
            qs = jnp.concatenate([qd * hm_ref[h:h + 1, :] for h in range(HG_HEADS)], axis=0)
            ks = jnp.exp2(ends[s] - ck[sub, :]).astype(BF16)
            sc = _dot_nt(qs.astype(BF16), ks)
            pv = _dot(sc.astype(BF16), v[sub, :])
            add = pv[0:n, :] * hm_ref[0:1, :]
            for h in range(1, HG_HEADS):
                add = add + pv[h * n:(h + 1) * n, :] * hm_ref[h:h + 1, :]
            o = jnp.concatenate([o[:(s + 1) * HG_SUB, :], o[after, :] + add], axis=0)
        o_ref[rows, :] = o_ref[rows, :] + o
        return carry

    lax.fori_loop(0, seq // CHUNK, carry_state, 0, unroll=8)

    def post(i, carry):
        rows = pl.ds(pl.multiple_of(i * HG_TILE, HG_TILE), HG_TILE)
        o = o_ref[rows, :]
        g = p_ref[rows, 768:1024]
        ms = _dot_exact_rhs(o * o, onesbd_ref[...]) * (1.0 / HG_DK)
        o_ref[rows, :] = o * lax.rsqrt(ms + EPS) * gain * (g * _sigmoid(g))
        return carry

    lax.fori_loop(0, seq // HG_TILE, post, 0)


def _hgrn(proj, hg_vec, layer, batch, seq):
    bd = _ones_bd(64)
    i = np.arange(HG_TILE)
    tri = jnp.asarray((i[:, None] // CHUNK == i[None, :] // CHUNK) & (i[:, None] >= i[None, :]), BF16)
    return pl.pallas_call(
        _hgrn_kernel,
        grid=(batch,),
        in_specs=[
            pl.BlockSpec((seq, 1024), lambda b: (b, COL_HG // 1024)),
            _layer_spec((3, D_BRANCH), layer),
            _const_spec((HG_TILE, HG_TILE)), _const_spec((D_BRANCH, D_BRANCH)),
            _const_spec((D_BRANCH, D_BRANCH)), _const_spec((4, D_BRANCH)),
        ],
        out_specs=pl.BlockSpec((seq, D_BRANCH), lambda b: (b, 0)),
        out_shape=jax.ShapeDtypeStruct((batch * seq, D_BRANCH), F32),
        scratch_shapes=[
            pltpu.VMEM((D_BRANCH, D_BRANCH), F32),
            pltpu.VMEM((seq, D_BRANCH), F32),
            pltpu.VMEM((seq, D_BRANCH), F32),
            pltpu.VMEM((CHUNK // HG_SUB * HG_SUB_STACK, D_BRANCH), BF16),
        ],
        compiler_params=_cparams(),
        name="hgrn",
    )(proj, hg_vec, tri, bd, bd.astype(F32), _head_masks())


def _ssd_kernel(xbc_ref, z_ref, dt_ref, cw_ref, cb_ref, v256_ref,
                tri_ref, u_ref, gm_ref, hm_ref, o_ref, sn_ref):
    seq = xbc_ref.shape[0]
    dx_ref = v256_ref.at[0:1, :]
    ng_ref = v256_ref.at[1:2, :]
    dtb_ref = v256_ref.at[2:3, :]
    alog_ref = v256_ref.at[3:4, :]
    sn_ref[...] = jnp.zeros_like(sn_ref)
    lidx = lax.broadcasted_iota(jnp.int32, (CHUNK, D_BRANCH), 0)
    sidx = lax.broadcasted_iota(jnp.int32, (CHUNK, D_BRANCH), 1) % CHUNK
    causal = lidx >= sidx
    lane128 = lax.broadcasted_iota(jnp.int32, (1, 2 * SSD_DSTATE), 1)
    a_neg = -jnp.exp(alog_ref[...])

    def chunk(c, carry):
        r0 = pl.multiple_of(c * CHUNK, CHUNK)
        rows = pl.ds(r0, CHUNK)
        cur = xbc_ref[rows, :]
        prev = xbc_ref[pl.ds(pl.multiple_of(jnp.maximum(r0 - 8, 0), 8), 8), :]
        prev = prev * jnp.where(c > 0, 1.0, 0.0)
        ext = jnp.concatenate([prev, cur], axis=0)
        conv = cb_ref[...] + cur * cw_ref[SSD_CONV - 1:SSD_CONV, :]
        for s in range(1, SSD_CONV):
            sh = pltpu.roll(ext, s, axis=0)[8:8 + CHUNK, :]
            conv = conv + sh * cw_ref[SSD_CONV - 1 - s:SSD_CONV - s, :]
        xa = conv * _sigmoid(conv)
        xs = xa[:, 0:SSD_INNER]
        bm = xa[:, SSD_INNER:SSD_INNER + 128]
        cm = xa[:, SSD_INNER + 128:SSD_INNER + 256]

        dtx = _softplus(dt_ref[rows, :] + dtb_ref[...])
        adtx = dtx * a_neg
        cum = _dot_exact_lhs(tri_ref[...], jnp.concatenate([adtx, adtx * u_ref[...]], axis=1))
        acum = cum[:, 0:D_BRANCH]
        seg = cum[:, D_BRANCH:2 * D_BRANCH]
        lmat = jnp.exp(jnp.where(causal, seg, NEG_INF))
        xdt = xs * dtx

        bm16 = bm.astype(BF16)
        y = jnp.zeros((CHUNK, D_BRANCH), F32)
        for grp in range(SSD_GROUPS):
            gsel = (lane128 // SSD_DSTATE) == grp
            gmat = _dot_nt(jnp.where(gsel, cm, 0.0).astype(BF16), bm16)
            for h in range(grp * 2, grp * 2 + 2):
                mh = gmat * lmat[:, h * CHUNK:(h + 1) * CHUNK]
                y = y + _dot(mh.astype(BF16), (xdt * hm_ref[h:h + 1, :]).astype(BF16))

        sn_old = sn_ref[...]
        y = y + _dot(cm.astype(BF16), sn_old.astype(BF16)) * jnp.exp(acum)
        aend = acum[CHUNK - 1:CHUNK, :]
        upd = _dot_tn(bm16, (jnp.exp(aend - acum) * xdt).astype(BF16))
        sn_ref[...] = jnp.exp(aend) * sn_old + upd * gm_ref[...]

        y = y + dx_ref[...] * xs
        zz = z_ref[rows, :]
        y = y * (zz * _sigmoid(zz))
        halves = []
        for grp in range(SSD_GROUPS):
            yh = y[:, grp * 128:(grp + 1) * 128]
            halves.append(yh * lax.rsqrt(jnp.mean(yh * yh, axis=-1, keepdims=True) + EPS))
        o_ref[rows, :] = jnp.concatenate(halves, axis=1) * ng_ref[...]
        return carry

    lax.fori_loop(0, seq // CHUNK, chunk, 0, unroll=8)


def _ssd(proj, conv_wt, conv_b, v256, layer, batch, seq):
    full = _const_spec
    li = np.arange(CHUNK)
    u_t = np.tile((li[:, None] > li[None, :]).astype(np.float32), (1, SSD_HEADS))
    gm = ((np.arange(128) // 64)[:, None] == (np.arange(256) // 128)[None, :]).astype(np.float32)
    return pl.pallas_call(
        _ssd_kernel,
        grid=(batch,),
        in_specs=[
            pl.BlockSpec((seq, SSD_XBC), lambda b: (b, COL_XBC // SSD_XBC)),
            pl.BlockSpec((seq, SSD_INNER), lambda b: (b, COL_Z // SSD_INNER)),
            pl.BlockSpec((seq, D_BRANCH), lambda b: (b, COL_DT // D_BRANCH)),
            _layer_spec((SSD_CONV, SSD_XBC), layer), _layer_spec((1, SSD_XBC), layer),
            _layer_spec((4, D_BRANCH), layer),
            full((CHUNK, CHUNK)), full((CHUNK, D_BRANCH)),
            full((128, D_BRANCH)), full((4, D_BRANCH)),
        ],
        out_specs=pl.BlockSpec((seq, D_BRANCH), lambda b: (b, 0)),
        out_shape=jax.ShapeDtypeStruct((batch * seq, D_BRANCH), F32),
        scratch_shapes=[pltpu.VMEM((2 * SSD_DSTATE, D_BRANCH), F32)],
        compiler_params=_cparams(),
        name="ssd",
    )(proj, proj, proj, conv_wt, conv_b, v256,
      _tri64(), jnp.asarray(u_t), jnp.asarray(gm), _head_masks())


S5_RT = 256


def _gelu_tanh(x):
    return 0.5 * x * (1.0 + jnp.tanh(math.sqrt(2.0 / math.pi) * (x + 0.044715 * (x * x * x))))


def _s5_kernel(ua_ref, ub_ref, kbd_ref, wst_ref, wout_ref, lam_ref, d_ref, wglu_ref, o_ref,
               hloc_s, hprev_s, ya_s, yb_s):
    u_refs = (ua_ref, ub_ref)
    y_refs = (ya_s, yb_s)
    seq = ua_ref.shape[0]
    nchunk = seq // S5_L
    hs = S5_HALF_STATE
    rowmod = lax.broadcasted_iota(jnp.int32, (S5_RT, D_BRANCH), 0) % S5_L

    def tile(i, carry):
        rows = pl.ds(pl.multiple_of(i * S5_RT, S5_RT), S5_RT)
        ut = jnp.concatenate([ua_ref[rows, :], ub_ref[rows, :]], axis=1)
        acc = _dot(ut.astype(BF16), kbd_ref[0])
        for d in range(1, S5_L):
            ud = jnp.where(rowmod >= d, pltpu.roll(ut, d, axis=0), 0.0)
            acc = acc + _dot(ud.astype(BF16), kbd_ref[d])
        ya_s[rows, :] = acc[:, 0:128]
        yb_s[rows, :] = acc[:, 128:256]
        return carry

    lax.fori_loop(0, seq // S5_RT, tile, 0)

    for half in range(2):
        acc = None
        for s in range(S5_L):
            us = u_refs[half][pl.ds(s, nchunk, stride=S5_L), :]
            term = _dot(us.astype(BF16), wst_ref[half, s])
            acc = term if acc is None else acc + term
        hloc_s[half] = acc

    for half in range(2):
        lr = lam_ref[half, 0:1, :]
        li = lam_ref[half, 1:2, :]

        def step(c, carry):
            hr, hi = carry
            hprev_s[half, pl.ds(c, 1), 0:hs] = hr
            hprev_s[half, pl.ds(c, 1), hs:2 * hs] = hi
            loc = hloc_s[half, pl.ds(c, 1), :]
            return (lr * hr - li * hi + loc[:, 0:hs], lr * hi + li * hr + loc[:, hs:2 * hs])

        zero = jnp.zeros((1, hs), F32)
        lax.fori_loop(0, nchunk, step, (zero, zero))

    for half in range(2):
        hp = hprev_s[half].astype(BF16)
        for s in range(S5_L):
            srows = pl.ds(s, nchunk, stride=S5_L)
            y_refs[half][srows, :] = y_refs[half][srows, :] + _dot_nt(hp, wout_ref[half, s])

    def tail(i, carry):
        rows = pl.ds(pl.multiple_of(i * S5_RT, S5_RT), S5_RT)
        ut = jnp.concatenate([ua_ref[rows, :], ub_ref[rows, :]], axis=1)
        yt = jnp.concatenate([ya_s[rows, :], yb_s[rows, :]], axis=1)
        y = _gelu_tanh(yt + d_ref[...] * ut)
        o_ref[rows, :] = y * _sigmoid(_dot(y.astype(BF16), wglu_ref[...]))
        return carry

    lax.fori_loop(0, seq // S5_RT, tail, 0)


def _s5_weights(a_re, a_im, b_re, b_im, c_re, c_im, log_dt):
    hp = lax.Precision.HIGHEST
    step = jnp.exp(log_dt)[:, None]
    mag = jnp.exp(a_re * step)
    lam_re = mag * jnp.cos(a_im * step)
    lam_im = mag * jnp.sin(a_im * step)
    den = a_re * a_re + a_im * a_im
    num_re = lam_re - 1.0
    coef_re = (num_re * a_re + lam_im * a_im) / den
    coef_im = (lam_im * a_re - num_re * a_im) / den
    bb_re = coef_re[..., None] * b_re - coef_im[..., None] * b_im
    bb_im = coef_re[..., None] * b_im + coef_im[..., None] * b_re

    pw_re = [jnp.ones_like(lam_re)]
    pw_im = [jnp.zeros_like(lam_im)]
    for _ in range(S5_L):
        pr, pi = pw_re[-1], pw_im[-1]
        pw_re.append(pr * lam_re - pi * lam_im)
        pw_im.append(pr * lam_im + pi * lam_re)
    rev_re = jnp.stack(pw_re[S5_L - 1::-1])
    rev_im = jnp.stack(pw_im[S5_L - 1::-1])
    pw_re = jnp.stack(pw_re)
    pw_im = jnp.stack(pw_im)

    cp_re = c_re[None] * pw_re[:, :, None, :] - c_im[None] * pw_im[:, :, None, :]
    cp_im = c_re[None] * pw_im[:, :, None, :] + c_im[None] * pw_re[:, :, None, :]
    kd = (jnp.einsum('dgip,gpj->dgij', cp_re[:S5_L], bb_re, precision=hp)
          - jnp.einsum('dgip,gpj->dgij', cp_im[:S5_L], bb_im, precision=hp))
    eye_g = jnp.eye(S5_GROUPS, dtype=F32)
    kbd = (jnp.transpose(kd, (0, 1, 3, 2))[:, :, :, None, :] * eye_g[None, :, None, :, None])
    kbd = kbd.reshape(S5_L, D_BRANCH, D_BRANCH)

    def halves(t):
        return jnp.moveaxis(t.reshape((S5_L, 2, S5_HALF_G) + t.shape[2:]), 1, 0)

    bbt_re = jnp.swapaxes(bb_re, 1, 2)[None]
    bbt_im = jnp.swapaxes(bb_im, 1, 2)[None]
    rr, ri_ = rev_re[:, :, None, :], rev_im[:, :, None, :]
    wbt = jnp.stack([halves(rr * bbt_re - ri_ * bbt_im),
                     halves(rr * bbt_im + ri_ * bbt_re)], axis=4)
    cpn = jnp.stack([halves(cp_re[1:S5_L + 1]), halves(-cp_im[1:S5_L + 1])], axis=4)
    compact = jnp.stack([wbt, cpn]).reshape(2, 2, S5_L, S5_HALF_G, S5_GROUP_CH, 2 * S5_STATE)
    lam_l = jnp.stack([pw_re[S5_L].reshape(2, S5_HALF_STATE),
                       pw_im[S5_L].reshape(2, S5_HALF_STATE)], axis=1)
    return kbd.astype(BF16), compact, lam_l


def _s5_expand_kernel(c_ref, o_ref):
    o_ref[...] = jnp.zeros(o_ref.shape, BF16)
    for s in range(S5_L):
        for g in range(S5_HALF_G):
            blk = c_ref[s, g]
            for ri in range(2):
                lo = ri * S5_HALF_STATE + g * S5_STATE
                o_ref[s, g * S5_GROUP_CH:(g + 1) * S5_GROUP_CH, lo:lo + S5_STATE] = (
                    blk[:, ri * S5_STATE:(ri + 1) * S5_STATE].astype(BF16))


def _s5_expand(compact):
    depth = compact.shape[0]
    blk_in = (None, None, None, S5_L, S5_HALF_G, S5_GROUP_CH, 2 * S5_STATE)
    blk_out = (None, None, None, S5_L, S5_HALF_G * S5_GROUP_CH, 2 * S5_HALF_STATE)
    idx = lambda l, w, h: (l, w, h, 0, 0, 0, 0)
    return pl.pallas_call(
        _s5_expand_kernel,
        grid=(depth, 2, 2),
        in_specs=[pl.BlockSpec(blk_in, idx)],
        out_specs=pl.BlockSpec(blk_out, lambda l, w, h: (l, w, h, 0, 0, 0)),
        out_shape=jax.ShapeDtypeStruct((depth, 2, 2, S5_L, 128, 2 * S5_HALF_STATE), BF16),
        compiler_params=_cparams(3),
        name="s5_expand",
    )(compact)


def _s5(proj, kbd, wmats, lam_l, d_skip, w_glu, layer, batch, seq):
    nchunk = seq // S5_L
    wspec = lambda which: pl.BlockSpec(
        (None, None, 2, S5_L, 128, 2 * S5_HALF_STATE), lambda b: (layer, which, 0, 0, 0, 0),
        pipeline_mode=pl.Buffered(1))
    return pl.pallas_call(
        _s5_kernel,
        grid=(batch,),
        in_specs=[
            pl.BlockSpec((seq, 128), lambda b: (b, COL_U // 128)),
            pl.BlockSpec((seq, 128), lambda b: (b, COL_U // 128 + 1)),
            _layer_spec((S5_L, D_BRANCH, D_BRANCH), layer),
            wspec(0), wspec(1),
            _layer_spec((2, 2, S5_HALF_STATE), layer),
            _layer_spec((1, D_BRANCH), layer),
            _layer_spec((D_BRANCH, D_BRANCH), layer),
        ],
        out_specs=pl.BlockSpec((seq, D_BRANCH), lambda b: (b, 0)),
        out_shape=jax.ShapeDtypeStruct((batch * seq, D_BRANCH), F32),
        scratch_shapes=[
            pltpu.VMEM((2, nchunk, 2 * S5_HALF_STATE), F32),
            pltpu.VMEM((2, nchunk, 2 * S5_HALF_STATE), F32),
            pltpu.VMEM((seq, 128), F32),
            pltpu.VMEM((seq, 128), F32),
        ],
        compiler_params=_cparams(),
        name="s5",
    )(proj, proj, kbd, wmats, wmats, lam_l, d_skip, w_glu)


def _attn_kernel(q_ref, k_ref, v_ref, gains_ref, bias_ref, onesbd_ref, hm_ref, o_ref,
                 kp_s, vp_s):
    seq = q_ref.shape[0]
    qg_ref = gains_ref.at[0:1, :]
    kg_ref = gains_ref.at[1:2, :]
    kp_s[0:ATT_PAD, :] = jnp.zeros((ATT_PAD, D_BRANCH), BF16)
    vp_s[0:ATT_PAD, :] = jnp.zeros((ATT_PAD, D_BRANCH), BF16)
    rt = 256

    def prep(i, carry):
        rows = pl.ds(pl.multiple_of(i * rt, rt), rt)
        k = k_ref[rows, :]
        ms = _dot_exact_rhs(k * k, onesbd_ref[...]) * (1.0 / ATT_HEADDIM)
        kn = k * lax.rsqrt(ms + EPS) * kg_ref[...]
        prow = pl.ds(pl.multiple_of(ATT_PAD + i * rt, 64), rt)
        kp_s[prow, :] = kn.astype(BF16)
        vp_s[prow, :] = v_ref[rows, :].astype(BF16)
        return carry

    lax.fori_loop(0, seq // rt, prep, 0)

    pidx = lax.broadcasted_iota(jnp.int32, (ATT_HEADS * CHUNK, ATT_BAND), 1)
    scale = ATT_HEADDIM ** -0.5

    def chunk(c, carry):
        r0 = pl.multiple_of(c * CHUNK, CHUNK)
        q = q_ref[pl.ds(r0, CHUNK), :]
        ms = _dot_exact_rhs(q * q, onesbd_ref[...]) * (1.0 / ATT_HEADDIM)
        qn = q * lax.rsqrt(ms + EPS) * (qg_ref[...] * scale)
        qs = jnp.concatenate([qn * hm_ref[h:h + 1, :] for h in range(ATT_HEADS)], axis=0).astype(BF16)
        kb = kp_s[pl.ds(r0, ATT_BAND), :]
        vb = vp_s[pl.ds(r0, ATT_BAND), :]
        s = _dot_nt(qs, kb)
        first_valid = jnp.maximum(ATT_LEFT_CHUNKS + 1 - c, 1) * CHUNK
        s = jnp.where(pidx >= first_valid, s + bias_ref[...], NEG_INF)
        m = jnp.max(s, axis=-1, keepdims=True)
        e = jnp.exp(s - m)
        l = jnp.sum(e, axis=-1, keepdims=True)
        pv = _dot(e.astype(BF16), vb) / l
        out = pv[0:CHUNK, :] * hm_ref[0:1, :]
        for h in range(1, ATT_HEADS):
            out = out + pv[h * CHUNK:(h + 1) * CHUNK, :] * hm_ref[h:h + 1, :]
        o_ref[pl.ds(r0, CHUNK), :] = out
        return carry

    lax.fori_loop(0, seq // CHUNK, chunk, 0, unroll=8)


def _attn_bias(rel_bias):
    n = CHUNK + ATT_BAND + 1
    j = np.arange(n)
    delta = np.where(j < ATT_BAND + 2, -j, n - j)
    rel = np.clip(delta + (ATT_LEFT_CHUNKS + 1) * CHUNK, -ATT_MAX_REL, ATT_MAX_REL) + ATT_MAX_REL
    d = rel_bias[:, rel]
    flat = jnp.tile(d, (1, CHUNK))[:, :CHUNK * (n - 1)]
    bias = flat.reshape(ATT_HEADS, CHUNK, n - 1)[:, :, :ATT_BAND]
    return bias.reshape(ATT_HEADS * CHUNK, ATT_BAND)


def _attn(proj, gains, bias, layer, batch, seq):
    full = _const_spec
    colblk = lambda j: pl.BlockSpec((seq, D_BRANCH), lambda b: (b, COL_QKV // D_BRANCH + j))
    return pl.pallas_call(
        _attn_kernel,
        grid=(batch,),
        in_specs=[
            colblk(0), colblk(1), colblk(2),
            _layer_spec((2, D_BRANCH), layer),
            _layer_spec((ATT_HEADS * CHUNK, ATT_BAND), layer),
            full((D_BRANCH, D_BRANCH)), full((4, D_BRANCH)),
        ],
        out_specs=pl.BlockSpec((seq, D_BRANCH), lambda b: (b, 0)),
        out_shape=jax.ShapeDtypeStruct((batch * seq, D_BRANCH), F32),
        scratch_shapes=[
            pltpu.VMEM((ATT_PAD + seq, D_BRANCH), BF16),
            pltpu.VMEM((ATT_PAD + seq, D_BRANCH), BF16),
        ],
        compiler_params=_cparams(),
        name="attn",
    )(proj, proj, proj, gains, bias, _ones_bd(64), _head_masks())


def kernel(x, p, norm_mix, w_in, hg_lb_logits, hg_o_norm, ssd_conv_w, ssd_conv_b, ssd_dt_bias, ssd_A_log, ssd_D, ssd_norm, s5_A_re, s5_A_im, s5_B_re, s5_B_im, s5_C_re, s5_C_im, s5_D, s5_log_dt, s5_w_glu, att_q_norm, att_k_norm, att_rel_bias, w_branch, w_out, norm_ffn, w_ff1, w_ff2, w_ple, norm_ple, w_ple_gate):
    batch, seq, _ = x.shape
    depth = w_in.shape[0]
    t = batch * seq
    row3 = lambda a: a.reshape(depth, 1, -1)

    lb_all = jnp.cumsum(jax.nn.softmax(hg_lb_logits.astype(F32), axis=0), axis=0)
    lb_all = lb_all - lb_all[0:1]
    hg_vec = jnp.stack([jnp.log(lb_all), jnp.log1p(-lb_all), hg_o_norm], axis=1)
    spread = lambda v: jnp.repeat(v, SSD_HEADDIM, axis=1)
    ssd_v256 = jnp.stack([spread(ssd_D), ssd_norm, spread(ssd_dt_bias), spread(ssd_A_log)], axis=1)
    conv_wt = jnp.swapaxes(ssd_conv_w, 1, 2)
    kbd, s5_compact, lam_l = jax.vmap(_s5_weights)(s5_A_re, s5_A_im, s5_B_re, s5_B_im, s5_C_re,
                                                   s5_C_im, s5_log_dt)
    s5_mats = _s5_expand(s5_compact)
    att_gains = jnp.stack([jnp.tile(att_q_norm, (1, ATT_HEADS)), jnp.tile(att_k_norm, (1, ATT_HEADS))],
                          axis=1)
    att_bias = jax.vmap(_attn_bias)(att_rel_bias)
    w_mix, w_gate = _repack(w_in)
    w_branch16, w_out16, w_glu16 = w_branch.astype(BF16), w_out.astype(BF16), s5_w_glu.astype(BF16)
    w_ff1_16, w_ff2_16 = w_ff1.astype(BF16), w_ff2.astype(BF16)
    w_pg16, w_ple16 = w_ple_gate.astype(BF16), w_ple.astype(BF16)
    g_mix, g_ffn, g_ple = row3(norm_mix), row3(norm_ffn), row3(norm_ple)

    x2 = x.reshape(t, D_MODEL)
    p3 = p.reshape(depth, t, D_PLE)
    for i in range(depth):
        proj = _inproj(x2, g_mix, w_mix, i)
        y_a = _hgrn(proj, hg_vec, i, batch, seq)
        y_b = _ssd(proj, conv_wt, row3(ssd_conv_b), ssd_v256, i, batch, seq)
        y_c = _s5(proj, kbd, s5_mats, lam_l, row3(s5_D), w_glu16, i, batch, seq)
        y_d = _attn(proj, att_gains, att_bias, i, batch, seq)
        x2 = _merge(x2, y_a, y_b, y_c, y_d, g_mix, w_gate, w_branch16, w_out16, i)
        x2 = _ffn_ple(x2, p3, g_ffn, w_ff1_16, w_ff2_16, g_ple, w_pg16, w_ple16, i)
    return x2.reshape(batch, seq, D_MODEL)
```

```python
import math

import jax
import jax.numpy as jnp
import numpy as np
from jax import lax
from jax.experimental import pallas as pl
from jax.experimental.pallas import tpu as pltpu

F32 = jnp.float32
BF16 = jnp.bfloat16

D_MODEL = 1024
CHUNK = 64
D_PLE = 256
N_BRANCH = 4
D_BRANCH = 256
D_FF = 4096
EPS = 1e-6
NEG_INF = -1e30
LOG2E = math.log2(math.e)

HG_HEADS = 4
HG_DK = 64
SSD_HEADS = 4
SSD_HEADDIM = 64
SSD_GROUPS = 2
SSD_DSTATE = 64
SSD_CONV = 4
SSD_INNER = 256
SSD_XBC = 512
S5_GROUP_CH = 16
S5_GROUPS = 16
S5_STATE = 64
ATT_HEADS = 4
ATT_HEADDIM = 64
ATT_LEFT_CHUNKS = 8
ATT_MAX_REL = 128

COL_HG = 0
COL_XBC = 1024
COL_Z = 1536
COL_U = 1792
COL_QKV = 2048
COL_DT = 2816
N_MIX = 3072
DT_PAD = 128

W_HG, W_Z, W_XBC, W_DT, W_U, W_QKV, W_GATE = 0, 1024, 1280, 1792, 1796, 2052, 2820
N_IN = 6916

S5_L = 8
S5_HALF_G = 8
S5_HALF_STATE = S5_HALF_G * S5_STATE

ATT_BAND = (ATT_LEFT_CHUNKS + 2) * CHUNK
ATT_PAD = (ATT_LEFT_CHUNKS + 1) * CHUNK

VMEM_LIMIT = 56 * 1024 * 1024


def _cparams(n_axes=1):
    return pltpu.CompilerParams(
        dimension_semantics=("arbitrary",) * n_axes, vmem_limit_bytes=VMEM_LIMIT)


def _layer_spec(shape, layer):
    return pl.BlockSpec((None,) + tuple(shape), lambda *_: (layer,) + (0,) * len(shape),
                        pipeline_mode=pl.Buffered(1))


def _const_spec(shape):
    return pl.BlockSpec(tuple(shape), lambda *_: (0,) * len(shape), pipeline_mode=pl.Buffered(1))


def _dot(a, b):
    return jnp.dot(a, b, preferred_element_type=F32)


def _dot_nt(a, b):
    return lax.dot_general(a, b, (((1,), (1,)), ((), ())), preferred_element_type=F32)


def _dot_tn(a, b):
    return lax.dot_general(a, b, (((0,), (0,)), ((), ())), preferred_element_type=F32)


def _split3(a):
    a1 = a.astype(BF16)
    r1 = a - a1.astype(F32)
    a2 = r1.astype(BF16)
    r2 = r1 - a2.astype(F32)
    return a1, a2, r2.astype(BF16)


def _dot_exact_rhs(a, m):
    a1, a2, a3 = _split3(a)
    return _dot(a1, m) + _dot(a2, m) + _dot(a3, m)


def _dot_exact_lhs(m, a):
    a1, a2, a3 = _split3(a)
    return _dot(m, a1) + _dot(m, a2) + _dot(m, a3)


def _sigmoid(x):
    return 1.0 / (1.0 + jnp.exp(-x))


def _softplus(x):
    return jnp.maximum(x, 0.0) + jnp.log1p(jnp.exp(-jnp.abs(x)))


def _rms_rows(x, g):
    return x * lax.rsqrt(jnp.mean(x * x, axis=-1, keepdims=True) + EPS) * g


REPACK_TN = 256
MIX_TILE_SRC = (W_HG, W_HG + 256, W_HG + 512, W_HG + 768, W_XBC, W_XBC + 256, W_Z, W_U,
                W_QKV, W_QKV + 256, W_QKV + 512, W_DT)
DT_TILE = COL_DT // REPACK_TN


def _repack_mix_kernel(src_ref, w_ref, sel_ref, o_ref):
    del src_ref
    j = pl.program_id(0)
    for l in range(o_ref.shape[0]):
        wt = w_ref[:, l, :].T.astype(BF16)

        @pl.when(j != DT_TILE)
        def _():
            o_ref[l] = wt

        @pl.when(j == DT_TILE)
        def _():
            o_ref[l] = _dot(wt, sel_ref[...]).astype(BF16)


def _repack_gate_kernel(w_ref, o_ref):
    for l in range(o_ref.shape[0]):
        o_ref[l] = w_ref[:, l, :].T.astype(BF16)


def _repack(w_in):
    depth = w_in.shape[0]
    w_t = jnp.transpose(w_in, (2, 0, 1))
    sel = np.zeros((REPACK_TN, D_BRANCH), np.float32)
    for h in range(SSD_HEADS):
        sel[h, h * 64:(h + 1) * 64] = 1.0
    w_mix = pl.pallas_call(
        _repack_mix_kernel,
        grid_spec=pltpu.PrefetchScalarGridSpec(
            num_scalar_prefetch=1,
            grid=(N_MIX // REPACK_TN,),
            in_specs=[pl.BlockSpec((pl.Element(REPACK_TN), pl.Element(depth), pl.Element(D_MODEL)),
                                   lambda j, src: (src[j], 0, 0)),
                      pl.BlockSpec((REPACK_TN, D_BRANCH), lambda j, src: (0, 0))],
            out_specs=pl.BlockSpec((depth, D_MODEL, REPACK_TN), lambda j, src: (0, 0, j)),
        ),
        out_shape=jax.ShapeDtypeStruct((depth, D_MODEL, N_MIX), BF16),
        compiler_params=_cparams(),
        name="repack_mix",
    )(jnp.asarray(MIX_TILE_SRC, jnp.int32), w_t, jnp.asarray(sel, BF16))
    n_gate = N_BRANCH * D_MODEL
    w_gate = pl.pallas_call(
        _repack_gate_kernel,
        grid=(n_gate // REPACK_TN,),
        in_specs=[pl.BlockSpec((pl.Element(REPACK_TN), pl.Element(depth), pl.Element(D_MODEL)),
                               lambda j: (W_GATE + j * REPACK_TN, 0, 0))],
        out_specs=pl.BlockSpec((depth, D_MODEL, REPACK_TN), lambda j: (0, 0, j)),
        out_shape=jax.ShapeDtypeStruct((depth, D_MODEL, n_gate), BF16),
        compiler_params=_cparams(),
        name="repack_gate",
    )(w_t)
    return w_mix, w_gate


def _inproj_kernel(x_ref, g_ref, w_ref, o_ref):
    h = _rms_rows(x_ref[...], g_ref[...])
    o_ref[...] = _dot(h.astype(BF16), w_ref[...])


def _inproj(x2, g, w_mix, layer, tm=1024):
    t = x2.shape[0]
    return pl.pallas_call(
        _inproj_kernel,
        grid=(t // tm,),
        in_specs=[
            pl.BlockSpec((tm, D_MODEL), lambda i: (i, 0)),
            _layer_spec((1, D_MODEL), layer),
            _layer_spec((D_MODEL, N_MIX), layer),
        ],
        out_specs=pl.BlockSpec((tm, N_MIX), lambda i: (i, 0)),
        out_shape=jax.ShapeDtypeStruct((t, N_MIX), F32),
        compiler_params=_cparams(),
        name="inproj",
    )(x2, g, w_mix)


def _merge_kernel(x_ref, ya_ref, yb_ref, yc_ref, yd_ref, g_ref, wg_ref, wb_ref, wo_ref, o_ref):
    x = x_ref[...]
    h = _rms_rows(x, g_ref[...]).astype(BF16)
    merged = None
    for m, y_ref in enumerate((ya_ref, yb_ref, yc_ref, yd_ref)):
        gate = _sigmoid(_dot(h, wg_ref[:, m * D_MODEL:(m + 1) * D_MODEL]))
        term = gate * _dot(y_ref[...].astype(BF16), wb_ref[m])
        merged = term if merged is None else merged + term
    o_ref[...] = x + _dot(merged.astype(BF16), wo_ref[...])


def _merge(x2, ya, yb, yc, yd, g, w_gate, w_branch, w_out, layer, tm=512):
    t = x2.shape[0]
    row = lambda i: (i, 0)
    return pl.pallas_call(
        _merge_kernel,
        grid=(t // tm,),
        in_specs=[
            pl.BlockSpec((tm, D_MODEL), row),
            pl.BlockSpec((tm, D_BRANCH), row),
            pl.BlockSpec((tm, D_BRANCH), row),
            pl.BlockSpec((tm, D_BRANCH), row),
            pl.BlockSpec((tm, D_BRANCH), row),
            _layer_spec((1, D_MODEL), layer),
            _layer_spec((D_MODEL, N_BRANCH * D_MODEL), layer),
            _layer_spec((N_BRANCH, D_BRANCH, D_MODEL), layer),
            _layer_spec((D_MODEL, D_MODEL), layer),
        ],
        out_specs=pl.BlockSpec((tm, D_MODEL), row),
        out_shape=jax.ShapeDtypeStruct((t, D_MODEL), F32),
        compiler_params=_cparams(),
        name="merge",
    )(x2, ya, yb, yc, yd, g, w_gate, w_branch, w_out)


def _ffn_ple_kernel(x_ref, p_ref, g1_ref, w1_ref, w2_ref, g2_ref, wg_ref, wp_ref, o_ref):
    x = x_ref[...]
    h = _rms_rows(x, g1_ref[...]).astype(BF16)
    acc = x
    tk = 1024
    for j in range(D_FF // tk):
        a = jnp.maximum(_dot(h, w1_ref[:, j * tk:(j + 1) * tk]), 0.0)
        acc = acc + _dot((a * a).astype(BF16), w2_ref[j * tk:(j + 1) * tk, :])
    h2 = _rms_rows(acc, g2_ref[...]).astype(BF16)
    gate = _sigmoid(_dot(h2, wg_ref[...]))
    o_ref[...] = acc + _dot(p_ref[...].astype(BF16), wp_ref[...]) * gate


def _ffn_ple(x2, p3, g_ffn, w1, w2, g_ple, w_gate, w_ple, layer, tm=1024):
    t = x2.shape[0]
    return pl.pallas_call(
        _ffn_ple_kernel,
        grid=(t // tm,),
        in_specs=[
            pl.BlockSpec((tm, D_MODEL), lambda i: (i, 0)),
            pl.BlockSpec((None, tm, D_PLE), lambda i: (layer, i, 0)),
            _layer_spec((1, D_MODEL), layer),
            _layer_spec((D_MODEL, D_FF), layer),
            _layer_spec((D_FF, D_MODEL), layer),
            _layer_spec((1, D_MODEL), layer),
            _layer_spec((D_MODEL, D_MODEL), layer),
            _layer_spec((D_PLE, D_MODEL), layer),
        ],
        out_specs=pl.BlockSpec((tm, D_MODEL), lambda i: (i, 0)),
        out_shape=jax.ShapeDtypeStruct((t, D_MODEL), F32),
        compiler_params=_cparams(),
        name="ffn_ple",
    )(x2, p3, g_ffn, w1, w2, g_ple, w_gate, w_ple)


def _tri64():
    i = np.arange(CHUNK)
    return jnp.asarray(i[:, None] >= i[None, :], BF16)


def _ones_bd(block, n=D_BRANCH):
    i = np.arange(n) // block
    return jnp.asarray(i[:, None] == i[None, :], BF16)


def _head_masks():
    lane = np.arange(D_BRANCH) // 64
    return jnp.asarray(lane[None, :] == np.arange(4)[:, None], F32)


HG_SUB = 16
HG_TILE = 256
HG_SUB_STACK = 8 * HG_SUB + 4 * HG_SUB


def _hgrn_kernel(p_ref, vec_ref, tri_ref, onesbd_ref, bdmask_ref, hm_ref, o_ref, st_ref, bl_s, ck_s, w_s):
    seq = p_ref.shape[0]
    half = HG_SUB // 2
    loglb = vec_ref[0:1, :]
    log1mlb = vec_ref[1:2, :]
    gain = vec_ref[2:3, :]

    def pre(i, carry):
        rows = pl.ds(pl.multiple_of(i * HG_TILE, HG_TILE), HG_TILE)
        z = p_ref[rows, 256:512]
        ls = jnp.minimum(z, 0.0) - jnp.log1p(jnp.exp(-jnp.abs(z)))
        y = log1mlb + ls
        lf = jnp.maximum(loglb, y) + jnp.log1p(jnp.exp(-jnp.abs(loglb - y)))
        bl2 = _dot_exact_lhs(tri_ref[...], lf) * LOG2E
        bl_s[rows, :] = bl2
        ck_s[rows, :] = bl2 - (log1mlb + (ls - z)) * LOG2E
        return carry

    lax.fori_loop(0, seq // HG_TILE, pre, 0)

    row16 = lax.broadcasted_iota(jnp.int32, (HG_SUB, D_BRANCH), 0)
    row8 = lax.broadcasted_iota(jnp.int32, (half, D_BRANCH), 0)

    def chunk(c, carry):
        r0 = c * CHUNK
        subs = []
        for s in range(CHUNK // HG_SUB):
            rs = pl.multiple_of(r0 + s * HG_SUB, HG_SUB)
            rows = pl.ds(rs, HG_SUB)
            q = p_ref[rows, 0:256]
            bl = bl_s[rows, :]
            base = s * HG_SUB_STACK
            for j in range(half):
                d = jnp.where(row16 >= j, bl - ck_s[pl.ds(rs + j, 1), :], NEG_INF)
                w_s[base + j * HG_SUB:base + (j + 1) * HG_SUB, :] = (q * jnp.exp2(d)).astype(BF16)
            q_hi = q[half:, :]
            bl_hi = bl[half:, :]
            for jp in range(half // 2):
                pieces = []
                for j in (half + 2 * jp, half + 2 * jp + 1):
                    d = jnp.where(row8 >= j - half, bl_hi - ck_s[pl.ds(rs + j, 1), :], NEG_INF)
                    pieces.append(q_hi * jnp.exp2(d))
                off = base + half * HG_SUB + jp * HG_SUB
                w_s[off:off + HG_SUB, :] = jnp.concatenate(pieces, axis=0).astype(BF16)
            subs.append((rs, rows, q, bl))

        r = _dot(w_s[...], onesbd_ref[...])

        for s, (rs, rows, q, bl) in enumerate(subs):
            base = s * HG_SUB_STACK
            o_lo = jnp.zeros((half, D_BRANCH), F32)
            o_hi = jnp.zeros((half, D_BRANCH), F32)
            for j in range(HG_SUB):
                vj = p_ref[pl.ds(rs + j, 1), 512:768]
                if j < half:
                    o_lo = o_lo + r[base + j * HG_SUB:base + j * HG_SUB + half, :] * vj
                    o_hi = o_hi + r[base + j * HG_SUB + half:base + (j + 1) * HG_SUB, :] * vj
                else:
                    off = base + half * HG_SUB + (j - half) * half
                    o_hi = o_hi + r[off:off + half, :] * vj
            o_ref[rows, :] = jnp.concatenate([o_lo, o_hi], axis=0)
        return carry

    lax.fori_loop(0, seq // CHUNK, chunk, 0, unroll=2)

    st_ref[...] = jnp.zeros_like(st_ref)
    nsub = CHUNK // HG_SUB

    def carry_state(c, carry):
        rows = pl.ds(pl.multiple_of(c * CHUNK, CHUNK), CHUNK)
        q = p_ref[rows, 0:256]
        v = p_ref[rows, 512:768].astype(BF16)
        b = bl_s[rows, :]
        ck = ck_s[rows, :]
        ends = [b[(s + 1) * HG_SUB - 1:(s + 1) * HG_SUB, :] for s in range(nsub)]
        st = st_ref[...]
        o = _dot_nt((q * jnp.exp2(b)).astype(BF16), st.astype(BF16))
        k_end = jnp.exp2(ends[-1] - ck).astype(BF16)
        st_ref[...] = st * jnp.exp2(ends[-1]) + _dot_tn(v, k_end) * bdmask_ref[...]
        for s in range(nsub - 1):
            sub = slice(s * HG_SUB, (s + 1) * HG_SUB)
            after = slice((s + 1) * HG_SUB, CHUNK)
            n = CHUNK - (s + 1) * HG_SUB
            qd = q[after, :] * jnp.exp2(b[after, :] - ends[s])
            qs = jnp.concatenate([qd * hm_ref[h:h + 1, :] for h in range(HG_HEADS)], axis=0)
            ks = jnp.exp2(ends[s] - ck[sub, :]).astype(BF16)
            sc = _dot_nt(qs.astype(BF16), ks)
            pv = _dot(sc.astype(BF16), v[sub, :])
            add = pv[0:n, :] * hm_ref[0:1, :]
            for h in range(1, HG_HEADS):
                add = add + pv[h * n:(h + 1) * n, :] * hm_ref[h:h + 1, :]
            o = jnp.concatenate([o[:(s + 1) * HG_SUB, :], o[after, :] + add], axis=0)
        o_ref[rows, :] = o_ref[rows, :] + o
        return carry

    lax.fori_loop(0, seq // CHUNK, carry_state, 0, unroll=8)

    def post(i, carry):
        rows = pl.ds(pl.multiple_of(i * HG_TILE, HG_TILE), HG_TILE)
        o = o_ref[rows, :]
        g = p_ref[rows, 768:1024]
        ms = _dot_exact_rhs(o * o, onesbd_ref[...]) * (1.0 / HG_DK)
        o_ref[rows, :] = o * lax.rsqrt(ms + EPS) * gain * (g * _sigmoid(g))
        return carry

    lax.fori_loop(0, seq // HG_TILE, post, 0)


def _hgrn(proj, hg_vec, layer, batch, seq):
    bd = _ones_bd(64)
    i = np.arange(HG_TILE)
    tri = jnp.asarray((i[:, None] // CHUNK == i[None, :] // CHUNK) & (i[:, None] >= i[None, :]), BF16)
    return pl.pallas_call(
        _hgrn_kernel,
        grid=(batch,),
        in_specs=[
            pl.BlockSpec((seq, 1024), lambda b: (b, COL_HG // 1024)),
            _layer_spec((3, D_BRANCH), layer),
            _const_spec((HG_TILE, HG_TILE)), _const_spec((D_BRANCH, D_BRANCH)),
            _const_spec((D_BRANCH, D_BRANCH)), _const_spec((4, D_BRANCH)),
        ],
        out_specs=pl.BlockSpec((seq, D_BRANCH), lambda b: (b, 0)),
        out_shape=jax.ShapeDtypeStruct((batch * seq, D_BRANCH), F32),
        scratch_shapes=[
            pltpu.VMEM((D_BRANCH, D_BRANCH), F32),
            pltpu.VMEM((seq, D_BRANCH), F32),
            pltpu.VMEM((seq, D_BRANCH), F32),
            pltpu.VMEM((CHUNK // HG_SUB * HG_SUB_STACK, D_BRANCH), BF16),
        ],
        compiler_params=_cparams(),
        name="hgrn",
    )(proj, hg_vec, tri, bd, bd.astype(F32), _head_masks())


def _ssd_kernel(xbc_ref, z_ref, dt_ref, cw_ref, cb_ref, v256_ref,
                tri_ref, u_ref, gm_ref, hm_ref, o_ref, sn_ref):
    seq = xbc_ref.shape[0]
    dx_ref = v256_ref.at[0:1, :]
    ng_ref = v256_ref.at[1:2, :]
    dtb_ref = v256_ref.at[2:3, :]
    alog_ref = v256_ref.at[3:4, :]
    sn_ref[...] = jnp.zeros_like(sn_ref)
    lidx = lax.broadcasted_iota(jnp.int32, (CHUNK, D_BRANCH), 0)
    sidx = lax.broadcasted_iota(jnp.int32, (CHUNK, D_BRANCH), 1) % CHUNK
    causal = lidx >= sidx
    lane128 = lax.broadcasted_iota(jnp.int32, (1, 2 * SSD_DSTATE), 1)
    a_neg = -jnp.exp(alog_ref[...])

    def chunk(c, carry):
        r0 = pl.multiple_of(c * CHUNK, CHUNK)
        rows = pl.ds(r0, CHUNK)
        cur = xbc_ref[rows, :]
        prev = xbc_ref[pl.ds(pl.multiple_of(jnp.maximum(r0 - 8, 0), 8), 8), :]
        prev = prev * jnp.where(c > 0, 1.0, 0.0)
        ext = jnp.concatenate([prev, cur], axis=0)
        conv = cb_ref[...] + cur * cw_ref[SSD_CONV - 1:SSD_CONV, :]
        for s in range(1, SSD_CONV):
            sh = pltpu.roll(ext, s, axis=0)[8:8 + CHUNK, :]
            conv = conv + sh * cw_ref[SSD_CONV - 1 - s:SSD_CONV - s, :]
        xa = conv * _sigmoid(conv)
        xs = xa[:, 0:SSD_INNER]
        bm = xa[:, SSD_INNER:SSD_INNER + 128]
        cm = xa[:, SSD_INNER + 128:SSD_INNER + 256]

        dtx = _softplus(dt_ref[rows, :] + dtb_ref[...])
        adtx = dtx * a_neg
        cum = _dot_exact_lhs(tri_ref[...], jnp.concatenate([adtx, adtx * u_ref[...]], axis=1))
        acum = cum[:, 0:D_BRANCH]
        seg = cum[:, D_BRANCH:2 * D_BRANCH]
        lmat = jnp.exp(jnp.where(causal, seg, NEG_INF))
        xdt = xs * dtx

        bm16 = bm.astype(BF16)
        bm2 = jnp.concatenate([bm16, bm16], axis=0)
        gcat = jnp.concatenate(
            [_dot_nt(jnp.where((lane128 // SSD_DSTATE) == grp, cm, 0.0).astype(BF16), bm2)
             for grp in range(SSD_GROUPS)], axis=1)
        xbd = jnp.concatenate([xdt * hm_ref[h:h + 1, :] for h in range(SSD_HEADS)], axis=0)
        y = _dot((gcat * lmat).astype(BF16), xbd.astype(BF16))

        sn_old = sn_ref[...]
        y = y + _dot(cm.astype(BF16), sn_old.astype(BF16)) * jnp.exp(acum)
        aend = acum[CHUNK - 1:CHUNK, :]
        upd = _dot_tn(bm16, (jnp.exp(aend - acum) * xdt).astype(BF16))
        sn_ref[...] = jnp.exp(aend) * sn_old + upd * gm_ref[...]

        y = y + dx_ref[...] * xs
        zz = z_ref[rows, :]
        y = y * (zz * _sigmoid(zz))
        halves = []
        for grp in range(SSD_GROUPS):
            yh = y[:, grp * 128:(grp + 1) * 128]
            halves.append(yh * lax.rsqrt(jnp.mean(yh * yh, axis=-1, keepdims=True) + EPS))
        o_ref[rows, :] = jnp.concatenate(halves, axis=1) * ng_ref[...]
        return carry

    lax.fori_loop(0, seq // CHUNK, chunk, 0, unroll=8)


def _ssd(proj, conv_wt, conv_b, v256, layer, batch, seq):
    full = _const_spec
    li = np.arange(CHUNK)
    u_t = np.tile((li[:, None] > li[None, :]).astype(np.float32), (1, SSD_HEADS))
    gm = ((np.arange(128) // 64)[:, None] == (np.arange(256) // 128)[None, :]).astype(np.float32)
    return pl.pallas_call(
        _ssd_kernel,
        grid=(batch,),
        in_specs=[
            pl.BlockSpec((seq, SSD_XBC), lambda b: (b, COL_XBC // SSD_XBC)),
            pl.BlockSpec((seq, SSD_INNER), lambda b: (b, COL_Z // SSD_INNER)),
            pl.BlockSpec((seq, D_BRANCH), lambda b: (b, COL_DT // D_BRANCH)),
            _layer_spec((SSD_CONV, SSD_XBC), layer), _layer_spec((1, SSD_XBC), layer),
            _layer_spec((4, D_BRANCH), layer),
            full((CHUNK, CHUNK)), full((CHUNK, D_BRANCH)),
            full((128, D_BRANCH)), full((4, D_BRANCH)),
        ],
        out_specs=pl.BlockSpec((seq, D_BRANCH), lambda b: (b, 0)),
        out_shape=jax.ShapeDtypeStruct((batch * seq, D_BRANCH), F32),
        scratch_shapes=[pltpu.VMEM((2 * SSD_DSTATE, D_BRANCH), F32)],
        compiler_params=_cparams(),
        name="ssd",
    )(proj, proj, proj, conv_wt, conv_b, v256,
      _tri64(), jnp.asarray(u_t), jnp.asarray(gm), _head_masks())


S5_RT = 256


def _gelu_tanh(x):
    return 0.5 * x * (1.0 + jnp.tanh(math.sqrt(2.0 / math.pi) * (x + 0.044715 * (x * x * x))))


def _s5_kernel(ua_ref, ub_ref, kbd_ref, wst_ref, wout_ref, lam_ref, d_ref, wglu_ref, o_ref,
               hloc_s, hprev_s, ya_s, yb_s):
    u_refs = (ua_ref, ub_ref)
    y_refs = (ya_s, yb_s)
    seq = ua_ref.shape[0]
    nchunk = seq // S5_L
    hs = S5_HALF_STATE
    rowmod = lax.broadcasted_iota(jnp.int32, (S5_RT, D_BRANCH), 0) % S5_L

    def tile(i, carry):
        rows = pl.ds(pl.multiple_of(i * S5_RT, S5_RT), S5_RT)
        ut = jnp.concatenate([ua_ref[rows, :], ub_ref[rows, :]], axis=1)
        acc = _dot(ut.astype(BF16), kbd_ref[0])
        for d in range(1, S5_L):
            ud = jnp.where(rowmod >= d, pltpu.roll(ut, d, axis=0), 0.0)
            acc = acc + _dot(ud.astype(BF16), kbd_ref[d])
        ya_s[rows, :] = acc[:, 0:128]
        yb_s[rows, :] = acc[:, 128:256]
        return carry

    lax.fori_loop(0, seq // S5_RT, tile, 0)

    for half in range(2):
        acc = None
        for s in range(S5_L):
            us = u_refs[half][pl.ds(s, nchunk, stride=S5_L), :]
            term = _dot(us.astype(BF16), wst_ref[half, s])
            acc = term if acc is None else acc + term
        hloc_s[half] = acc

    lam = [(lam_ref[half, 0:1, :], lam_ref[half, 1:2, :]) for half in range(2)]

    def step(c, carry):
        out = []
        for half in range(2):
            hr, hi = carry[half]
            lr, li = lam[half]
            hprev_s[half, pl.ds(c, 1), 0:hs] = hr
            hprev_s[half, pl.ds(c, 1), hs:2 * hs] = hi
            loc = hloc_s[half, pl.ds(c, 1), :]
            out.append((lr * hr - li * hi + loc[:, 0:hs], lr * hi + li * hr + loc[:, hs:2 * hs]))
        return tuple(out)

    zero = jnp.zeros((1, hs), F32)
    lax.fori_loop(0, nchunk, step, ((zero, zero), (zero, zero)), unroll=2)

    for half in range(2):
        hp = hprev_s[half].astype(BF16)
        for s in range(S5_L):
            srows = pl.ds(s, nchunk, stride=S5_L)
            y_refs[half][srows, :] = y_refs[half][srows, :] + _dot_nt(hp, wout_ref[half, s])

    def tail(i, carry):
        rows = pl.ds(pl.multiple_of(i * S5_RT, S5_RT), S5_RT)
        ut = jnp.concatenate([ua_ref[rows, :], ub_ref[rows, :]], axis=1)
        yt = jnp.concatenate([ya_s[rows, :], yb_s[rows, :]], axis=1)
        y = _gelu_tanh(yt + d_ref[...] * ut)
        o_ref[rows, :] = y * _sigmoid(_dot(y.astype(BF16), wglu_ref[...]))
        return carry

    lax.fori_loop(0, seq // S5_RT, tail, 0)


def _s5_weights(a_re, a_im, b_re, b_im, c_re, c_im, log_dt):
    hp = lax.Precision.HIGHEST
    step = jnp.exp(log_dt)[:, None]
    mag = jnp.exp(a_re * step)
    lam_re = mag * jnp.cos(a_im * step)
    lam_im = mag * jnp.sin(a_im * step)
    den = a_re * a_re + a_im * a_im
    num_re = lam_re - 1.0
    coef_re = (num_re * a_re + lam_im * a_im) / den
    coef_im = (lam_im * a_re - num_re * a_im) / den
    bb_re = coef_re[..., None] * b_re - coef_im[..., None] * b_im
    bb_im = coef_re[..., None] * b_im + coef_im[..., None] * b_re

    pw_re = [jnp.ones_like(lam_re)]
    pw_im = [jnp.zeros_like(lam_im)]
    for _ in range(S5_L):
        pr, pi = pw_re[-1], pw_im[-1]
        pw_re.append(pr * lam_re - pi * lam_im)
        pw_im.append(pr * lam_im + pi * lam_re)
    rev_re = jnp.stack(pw_re[S5_L - 1::-1])
    rev_im = jnp.stack(pw_im[S5_L - 1::-1])
    pw_re = jnp.stack(pw_re)
    pw_im = jnp.stack(pw_im)

    bbt_re = jnp.swapaxes(bb_re, 1, 2)[None]
    bbt_im = jnp.swapaxes(bb_im, 1, 2)[None]
    cp_re = c_re[None] * pw_re[:, :, None, :] - c_im[None] * pw_im[:, :, None, :]
    cp_im = c_re[None] * pw_im[:, :, None, :] + c_im[None] * pw_re[:, :, None, :]
    kd = jnp.einsum('gjq,dgiq->dgji', jnp.concatenate([bbt_re[0], bbt_im[0]], axis=-1),
                    jnp.concatenate([cp_re[:S5_L], -cp_im[:S5_L]], axis=-1), precision=hp)

    def halves(t):
        return jnp.moveaxis(t.reshape((S5_L, 2, S5_HALF_G) + t.shape[2:]), 1, 0)

    rr, ri_ = rev_re[:, :, None, :], rev_im[:, :, None, :]
    wbt = jnp.stack([halves(rr * bbt_re - ri_ * bbt_im),
                     halves(rr * bbt_im + ri_ * bbt_re)], axis=4)
    cpn = jnp.stack([halves(cp_re[1:S5_L + 1]), halves(-cp_im[1:S5_L + 1])], axis=4)
    compact = jnp.stack([wbt, cpn]).reshape(2, 2, S5_L, S5_HALF_G, S5_GROUP_CH, 2 * S5_STATE)
    lam_l = jnp.stack([pw_re[S5_L].reshape(2, S5_HALF_STATE),
                       pw_im[S5_L].reshape(2, S5_HALF_STATE)], axis=1)
    return kd, compact, lam_l


def _s5_expand_kernel(c_ref, o_ref):
    o_ref[...] = jnp.zeros(o_ref.shape, BF16)
    for s in range(S5_L):
        for g in range(S5_HALF_G):
            blk = c_ref[s, g]
            for ri in range(2):
                lo = ri * S5_HALF_STATE + g * S5_STATE
                o_ref[s, g * S5_GROUP_CH:(g + 1) * S5_GROUP_CH, lo:lo + S5_STATE] = (
                    blk[:, ri * S5_STATE:(ri + 1) * S5_STATE].astype(BF16))


def _s5_kbd_kernel(k_ref, o_ref):
    o_ref[...] = jnp.zeros(o_ref.shape, BF16)
    for d in range(S5_L):
        for g in range(S5_GROUPS):
            lo = g * S5_GROUP_CH
            o_ref[d, lo:lo + S5_GROUP_CH, lo:lo + S5_GROUP_CH] = k_ref[d, g].astype(BF16)


def _s5_kbd(kd):
    depth = kd.shape[0]
    return pl.pallas_call(
        _s5_kbd_kernel,
        grid=(depth,),
        in_specs=[pl.BlockSpec((None, S5_L, S5_GROUPS, S5_GROUP_CH, S5_GROUP_CH), lambda l: (l, 0, 0, 0, 0))],
        out_specs=pl.BlockSpec((None, S5_L, D_BRANCH, D_BRANCH), lambda l: (l, 0, 0, 0)),
        out_shape=jax.ShapeDtypeStruct((depth, S5_L, D_BRANCH, D_BRANCH), BF16),
        compiler_params=_cparams(),
        name="s5_kbd",
    )(kd)


def _s5_expand(compact):
    depth = compact.shape[0]
    blk_in = (None, None, None, S5_L, S5_HALF_G, S5_GROUP_CH, 2 * S5_STATE)
    blk_out = (None, None, None, S5_L, S5_HALF_G * S5_GROUP_CH, 2 * S5_HALF_STATE)
    idx = lambda l, w, h: (l, w, h, 0, 0, 0, 0)
    return pl.pallas_call(
        _s5_expand_kernel,
        grid=(depth, 2, 2),
        in_specs=[pl.BlockSpec(blk_in, idx)],
        out_specs=pl.BlockSpec(blk_out, lambda l, w, h: (l, w, h, 0, 0, 0)),
        out_shape=jax.ShapeDtypeStruct((depth, 2, 2, S5_L, 128, 2 * S5_HALF_STATE), BF16),
        compiler_params=_cparams(3),
        name="s5_expand",
    )(compact)


def _s5(proj, kbd, wmats, lam_l, d_skip, w_glu, layer, batch, seq):
    nchunk = seq // S5_L
    wspec = lambda which: pl.BlockSpec(
        (None, None, 2, S5_L, 128, 2 * S5_HALF_STATE), lambda b: (layer, which, 0, 0, 0, 0),
        pipeline_mode=pl.Buffered(1))
    return pl.pallas_call(
        _s5_kernel,
        grid=(batch,),
        in_specs=[
            pl.BlockSpec((seq, 128), lambda b: (b, COL_U // 128)),
            pl.BlockSpec((seq, 128), lambda b: (b, COL_U // 128 + 1)),
            _layer_spec((S5_L, D_BRANCH, D_BRANCH), layer),
            wspec(0), wspec(1),
            _layer_spec((2, 2, S5_HALF_STATE), layer),
            _layer_spec((1, D_BRANCH), layer),
            _layer_spec((D_BRANCH, D_BRANCH), layer),
        ],
        out_specs=pl.BlockSpec((seq, D_BRANCH), lambda b: (b, 0)),
        out_shape=jax.ShapeDtypeStruct((batch * seq, D_BRANCH), F32),
        scratch_shapes=[
            pltpu.VMEM((2, nchunk, 2 * S5_HALF_STATE), F32),
            pltpu.VMEM((2, nchunk, 2 * S5_HALF_STATE), F32),
            pltpu.VMEM((seq, 128), F32),
            pltpu.VMEM((seq, 128), F32),
        ],
        compiler_params=_cparams(),
        name="s5",
    )(proj, proj, kbd, wmats, wmats, lam_l, d_skip, w_glu)


def _attn_kernel(q_ref, k_ref, v_ref, gains_ref, bias_ref, onesbd_ref, hm_ref, o_ref,
                 kp_s, vp_s):
    seq = q_ref.shape[0]
    qg_ref = gains_ref.at[0:1, :]
    kg_ref = gains_ref.at[1:2, :]
    kp_s[0:ATT_PAD, :] = jnp.zeros((ATT_PAD, D_BRANCH), BF16)
    vp_s[0:ATT_PAD, :] = jnp.zeros((ATT_PAD, D_BRANCH), BF16)
    rt = 256

    def prep(i, carry):
        rows = pl.ds(pl.multiple_of(i * rt, rt), rt)
        k = k_ref[rows, :]
        ms = _dot_exact_rhs(k * k, onesbd_ref[...]) * (1.0 / ATT_HEADDIM)
        kn = k * lax.rsqrt(ms + EPS) * kg_ref[...]
        prow = pl.ds(pl.multiple_of(ATT_PAD + i * rt, 64), rt)
        kp_s[prow, :] = kn.astype(BF16)
        vp_s[prow, :] = v_ref[rows, :].astype(BF16)
        return carry

    lax.fori_loop(0, seq // rt, prep, 0)

    pidx = lax.broadcasted_iota(jnp.int32, (ATT_HEADS * CHUNK, ATT_BAND), 1)
    scale = ATT_HEADDIM ** -0.5

    def chunk(c, carry):
        r0 = pl.multiple_of(c * CHUNK, CHUNK)
        q = q_ref[pl.ds(r0, CHUNK), :]
        ms = _dot_exact_rhs(q * q, onesbd_ref[...]) * (1.0 / ATT_HEADDIM)
        qn = q * lax.rsqrt(ms + EPS) * (qg_ref[...] * scale)
        qs = jnp.concatenate([qn * hm_ref[h:h + 1, :] for h in range(ATT_HEADS)], axis=0).astype(BF16)
        kb = kp_s[pl.ds(r0, ATT_BAND), :]
        vb = vp_s[pl.ds(r0, ATT_BAND), :]
        s = _dot_nt(qs, kb)
        first_valid = jnp.maximum(ATT_LEFT_CHUNKS + 1 - c, 1) * CHUNK
        s = jnp.where(pidx >= first_valid, s + bias_ref[...], NEG_INF)
        m = jnp.max(s, axis=-1, keepdims=True)
        e = jnp.exp(s - m)
        l = jnp.sum(e, axis=-1, keepdims=True)
        pv = _dot(e.astype(BF16), vb) / l
        out = pv[0:CHUNK, :] * hm_ref[0:1, :]
        for h in range(1, ATT_HEADS):
            out = out + pv[h * CHUNK:(h + 1) * CHUNK, :] * hm_ref[h:h + 1, :]
        o_ref[pl.ds(r0, CHUNK), :] = out
        return carry

    lax.fori_loop(0, seq // CHUNK, chunk, 0, unroll=8)


def _attn_bias(rel_bias):
    n = CHUNK + ATT_BAND + 1
    j = np.arange(n)
    delta = np.where(j < ATT_BAND + 2, -j, n - j)
    rel = np.clip(delta + (ATT_LEFT_CHUNKS + 1) * CHUNK, -ATT_MAX_REL, ATT_MAX_REL) + ATT_MAX_REL
    d = rel_bias[:, rel]
    flat = jnp.tile(d, (1, CHUNK))[:, :CHUNK * (n - 1)]
    bias = flat.reshape(ATT_HEADS, CHUNK, n - 1)[:, :, :ATT_BAND]
    return bias.reshape(ATT_HEADS * CHUNK, ATT_BAND)


def _attn(proj, gains, bias, layer, batch, seq):
    full = _const_spec
    colblk = lambda j: pl.BlockSpec((seq, D_BRANCH), lambda b: (b, COL_QKV // D_BRANCH + j))
    return pl.pallas_call(
        _attn_kernel,
        grid=(batch,),
        in_specs=[
            colblk(0), colblk(1), colblk(2),
            _layer_spec((2, D_BRANCH), layer),
            _layer_spec((ATT_HEADS * CHUNK, ATT_BAND), layer),
            full((D_BRANCH, D_BRANCH)), full((4, D_BRANCH)),
        ],
        out_specs=pl.BlockSpec((seq, D_BRANCH), lambda b: (b, 0)),
        out_shape=jax.ShapeDtypeStruct((batch * seq, D_BRANCH), F32),
        scratch_shapes=[
            pltpu.VMEM((ATT_PAD + seq, D_BRANCH), BF16),
            pltpu.VMEM((ATT_PAD + seq, D_BRANCH), BF16),
        ],
        compiler_params=_cparams(),
        name="attn",
    )(proj, proj, proj, gains, bias, _ones_bd(64), _head_masks())


def kernel(x, p, norm_mix, w_in, hg_lb_logits, hg_o_norm, ssd_conv_w, ssd_conv_b, ssd_dt_bias, ssd_A_log, ssd_D, ssd_norm, s5_A_re, s5_A_im, s5_B_re, s5_B_im, s5_C_re, s5_C_im, s5_D, s5_log_dt, s5_w_glu, att_q_norm, att_k_norm, att_rel_bias, w_branch, w_out, norm_ffn, w_ff1, w_ff2, w_ple, norm_ple, w_ple_gate):
    batch, seq, _ = x.shape
    depth = w_in.shape[0]
    t = batch * seq
    row3 = lambda a: a.reshape(depth, 1, -1)

    lb_all = jnp.cumsum(jax.nn.softmax(hg_lb_logits.astype(F32), axis=0), axis=0)
    lb_all = lb_all - lb_all[0:1]
    hg_vec = jnp.stack([jnp.log(lb_all), jnp.log1p(-lb_all), hg_o_norm], axis=1)
    spread = lambda v: jnp.repeat(v, SSD_HEADDIM, axis=1)
    ssd_v256 = jnp.stack([spread(ssd_D), ssd_norm, spread(ssd_dt_bias), spread(ssd_A_log)], axis=1)
    conv_wt = jnp.swapaxes(ssd_conv_w, 1, 2)
    s5_kd, s5_compact, lam_l = jax.vmap(_s5_weights)(s5_A_re, s5_A_im, s5_B_re, s5_B_im, s5_C_re,
                                                     s5_C_im, s5_log_dt)
    s5_mats = _s5_expand(s5_compact)
    kbd = _s5_kbd(s5_kd)
    att_gains = jnp.stack([jnp.tile(att_q_norm, (1, ATT_HEADS)), jnp.tile(att_k_norm, (1, ATT_HEADS))],
                          axis=1)
    att_bias = jax.vmap(_attn_bias)(att_rel_bias)
    w_mix, w_gate = _repack(w_in)
    w_branch16, w_out16, w_glu16 = w_branch.astype(BF16), w_out.astype(BF16), s5_w_glu.astype(BF16)
    w_ff1_16, w_ff2_16 = w_ff1.astype(BF16), w_ff2.astype(BF16)
    w_pg16, w_ple16 = w_ple_gate.astype(BF16), w_ple.astype(BF16)
    g_mix, g_ffn, g_ple = row3(norm_mix), row3(norm_ffn), row3(norm_ple)

    x2 = x.reshape(t, D_MODEL)
    p3 = p.reshape(depth, t, D_PLE)
    for i in range(depth):
        proj = _inproj(x2, g_mix, w_mix, i)
        y_a = _hgrn(proj, hg_vec, i, batch, seq)
        y_b = _ssd(proj, conv_wt, row3(ssd_conv_b), ssd_v256, i, batch, seq)
        y_c = _s5(proj, kbd, s5_mats, lam_l, row3(s5_D), w_glu16, i, batch, seq)
        y_d = _attn(proj, att_gains, att_bias, i, batch, seq)
        x2 = _merge(x2, y_a, y_b, y_c, y_d, g_mix, w_gate, w_branch16, w_out16, i)
        x2 = _ffn_ple(x2, p3, g_ffn, w_ff1_16, w_ff2_16, g_ple, w_pg16, w_ple16, i)
    return x2.reshape(batch, seq, D_MODEL)
```

```python
import math

import jax
import jax.numpy as jnp
import numpy as np
from jax import lax
from jax.experimental import pallas as pl
from jax.experimental.pallas import tpu as pltpu

F32 = jnp.float32
BF16 = jnp.bfloat16

D_MODEL = 1024
CHUNK = 64
D_PLE = 256
N_BRANCH = 4
D_BRANCH = 256
D_FF = 4096
EPS = 1e-6
NEG_INF = -1e30
LOG2E = math.log2(math.e)

HG_HEADS = 4
HG_DK = 64
SSD_HEADS = 4
SSD_HEADDIM = 64
SSD_GROUPS = 2
SSD_DSTATE = 64
SSD_CONV = 4
SSD_INNER = 256
SSD_XBC = 512
S5_GROUP_CH = 16
S5_GROUPS = 16
S5_STATE = 64
ATT_HEADS = 4
ATT_HEADDIM = 64
ATT_LEFT_CHUNKS = 8
ATT_MAX_REL = 128

COL_HG = 0
COL_XBC = 1024
COL_Z = 1536
COL_U = 1792
COL_QKV = 2048
COL_DT = 2816
N_MIX = 3072
DT_PAD = 128

W_HG, W_Z, W_XBC, W_DT, W_U, W_QKV, W_GATE = 0, 1024, 1280, 1792, 1796, 2052, 2820
N_IN = 6916

S5_L = 8
S5_HALF_G = 8
S5_HALF_STATE = S5_HALF_G * S5_STATE

ATT_BAND = (ATT_LEFT_CHUNKS + 2) * CHUNK
ATT_PAD = (ATT_LEFT_CHUNKS + 1) * CHUNK

VMEM_LIMIT = 56 * 1024 * 1024


def _cparams(n_axes=1):
    return pltpu.CompilerParams(
        dimension_semantics=("arbitrary",) * n_axes, vmem_limit_bytes=VMEM_LIMIT)


def _layer_spec(shape, layer):
    return pl.BlockSpec((None,) + tuple(shape), lambda *_: (layer,) + (0,) * len(shape),
                        pipeline_mode=pl.Buffered(1))


def _const_spec(shape):
    return pl.BlockSpec(tuple(shape), lambda *_: (0,) * len(shape), pipeline_mode=pl.Buffered(1))


def _dot(a, b):
    return jnp.dot(a, b, preferred_element_type=F32)


def _dot_nt(a, b):
    return lax.dot_general(a, b, (((1,), (1,)), ((), ())), preferred_element_type=F32)


def _dot_tn(a, b):
    return lax.dot_general(a, b, (((0,), (0,)), ((), ())), preferred_element_type=F32)


def _split3(a):
    a1 = a.astype(BF16)
    r1 = a - a1.astype(F32)
    a2 = r1.astype(BF16)
    r2 = r1 - a2.astype(F32)
    return a1, a2, r2.astype(BF16)


def _dot_exact_rhs(a, m):
    a1, a2, a3 = _split3(a)
    return _dot(a1, m) + _dot(a2, m) + _dot(a3, m)


def _dot_exact_lhs(m, a):
    a1, a2, a3 = _split3(a)
    return _dot(m, a1) + _dot(m, a2) + _dot(m, a3)


def _sigmoid(x):
    return 1.0 / (1.0 + jnp.exp(-x))


def _softplus(x):
    return jnp.maximum(x, 0.0) + jnp.log(1.0 + jnp.exp(-jnp.abs(x)))


def _rms_rows(x, g):
    return x * lax.rsqrt(jnp.mean(x * x, axis=-1, keepdims=True) + EPS) * g


REPACK_TN = 256
MIX_TILE_SRC = (W_HG, W_HG + 256, W_HG + 512, W_HG + 768, W_XBC, W_XBC + 256, W_Z, W_U,
                W_QKV, W_QKV + 256, W_QKV + 512, W_DT)
DT_TILE = COL_DT // REPACK_TN


def _repack_mix_kernel(src_ref, w_ref, sel_ref, o_ref):
    del src_ref
    j = pl.program_id(0)
    for l in range(o_ref.shape[0]):
        wt = w_ref[:, l, :].T.astype(BF16)

        @pl.when(j != DT_TILE)
        def _():
            o_ref[l] = wt

        @pl.when(j == DT_TILE)
        def _():
            o_ref[l] = _dot(wt, sel_ref[...]).astype(BF16)


def _repack_gate_kernel(w_ref, o_ref):
    for l in range(o_ref.shape[0]):
        o_ref[l] = w_ref[:, l, :].T.astype(BF16)


def _repack(w_in):
    depth = w_in.shape[0]
    w_t = jnp.transpose(w_in, (2, 0, 1))
    sel = np.zeros((REPACK_TN, D_BRANCH), np.float32)
    for h in range(SSD_HEADS):
        sel[h, h * 64:(h + 1) * 64] = 1.0
    w_mix = pl.pallas_call(
        _repack_mix_kernel,
        grid_spec=pltpu.PrefetchScalarGridSpec(
            num_scalar_prefetch=1,
            grid=(N_MIX // REPACK_TN,),
            in_specs=[pl.BlockSpec((pl.Element(REPACK_TN), pl.Element(depth), pl.Element(D_MODEL)),
                                   lambda j, src: (src[j], 0, 0)),
                      pl.BlockSpec((REPACK_TN, D_BRANCH), lambda j, src: (0, 0))],
            out_specs=pl.BlockSpec((depth, D_MODEL, REPACK_TN), lambda j, src: (0, 0, j)),
        ),
        out_shape=jax.ShapeDtypeStruct((depth, D_MODEL, N_MIX), BF16),
        compiler_params=_cparams(),
        name="repack_mix",
    )(jnp.asarray(MIX_TILE_SRC, jnp.int32), w_t, jnp.asarray(sel, BF16))
    n_gate = N_BRANCH * D_MODEL
    w_gate = pl.pallas_call(
        _repack_gate_kernel,
        grid=(n_gate // REPACK_TN,),
        in_specs=[pl.BlockSpec((pl.Element(REPACK_TN), pl.Element(depth), pl.Element(D_MODEL)),
                               lambda j: (W_GATE + j * REPACK_TN, 0, 0))],
        out_specs=pl.BlockSpec((depth, D_MODEL, REPACK_TN), lambda j: (0, 0, j)),
        out_shape=jax.ShapeDtypeStruct((depth, D_MODEL, n_gate), BF16),
        compiler_params=_cparams(),
        name="repack_gate",
    )(w_t)
    return w_mix, w_gate


def _inproj_kernel(x_ref, g_ref, w_ref, o_ref):
    h = _rms_rows(x_ref[...], g_ref[...])
    o_ref[...] = _dot(h.astype(BF16), w_ref[...])


def _inproj(x2, g, w_mix, layer, tm=1024):
    t = x2.shape[0]
    return pl.pallas_call(
        _inproj_kernel,
        grid=(t // tm,),
        in_specs=[
            pl.BlockSpec((tm, D_MODEL), lambda i: (i, 0)),
            _layer_spec((1, D_MODEL), layer),
            _layer_spec((D_MODEL, N_MIX), layer),
        ],
        out_specs=pl.BlockSpec((tm, N_MIX), lambda i: (i, 0)),
        out_shape=jax.ShapeDtypeStruct((t, N_MIX), F32),
        compiler_params=_cparams(),
        name="inproj",
    )(x2, g, w_mix)


def _merge_kernel(x_ref, ya_ref, yb_ref, yc_ref, yd_ref, g_ref, wg_ref, wb_ref, wo_ref, o_ref):
    x = x_ref[...]
    h = _rms_rows(x, g_ref[...]).astype(BF16)
    merged = None
    for m, y_ref in enumerate((ya_ref, yb_ref, yc_ref, yd_ref)):
        gate = _sigmoid(_dot(h, wg_ref[:, m * D_MODEL:(m + 1) * D_MODEL]))
        term = gate * _dot(y_ref[...].astype(BF16), wb_ref[m])
        merged = term if merged is None else merged + term
    o_ref[...] = x + _dot(merged.astype(BF16), wo_ref[...])


def _merge(x2, ya, yb, yc, yd, g, w_gate, w_branch, w_out, layer, tm=512):
    t = x2.shape[0]
    row = lambda i: (i, 0)
    return pl.pallas_call(
        _merge_kernel,
        grid=(t // tm,),
        in_specs=[
            pl.BlockSpec((tm, D_MODEL), row),
            pl.BlockSpec((tm, D_BRANCH), row),
            pl.BlockSpec((tm, D_BRANCH), row),
            pl.BlockSpec((tm, D_BRANCH), row),
            pl.BlockSpec((tm, D_BRANCH), row),
            _layer_spec((1, D_MODEL), layer),
            _layer_spec((D_MODEL, N_BRANCH * D_MODEL), layer),
            _layer_spec((N_BRANCH, D_BRANCH, D_MODEL), layer),
            _layer_spec((D_MODEL, D_MODEL), layer),
        ],
        out_specs=pl.BlockSpec((tm, D_MODEL), row),
        out_shape=jax.ShapeDtypeStruct((t, D_MODEL), F32),
        compiler_params=_cparams(),
        name="merge",
    )(x2, ya, yb, yc, yd, g, w_gate, w_branch, w_out)


def _ffn_ple_kernel(x_ref, p_ref, g1_ref, w1_ref, w2_ref, g2_ref, wg_ref, wp_ref, o_ref):
    x = x_ref[...]
    h = _rms_rows(x, g1_ref[...]).astype(BF16)
    acc = x
    tk = 1024
    for j in range(D_FF // tk):
        a = jnp.maximum(_dot(h, w1_ref[:, j * tk:(j + 1) * tk]), 0.0)
        acc = acc + _dot((a * a).astype(BF16), w2_ref[j * tk:(j + 1) * tk, :])
    h2 = _rms_rows(acc, g2_ref[...]).astype(BF16)
    gate = _sigmoid(_dot(h2, wg_ref[...]))
    o_ref[...] = acc + _dot(p_ref[...].astype(BF16), wp_ref[...]) * gate


def _ffn_ple(x2, p3, g_ffn, w1, w2, g_ple, w_gate, w_ple, layer, tm=1024):
    t = x2.shape[0]
    return pl.pallas_call(
        _ffn_ple_kernel,
        grid=(t // tm,),
        in_specs=[
            pl.BlockSpec((tm, D_MODEL), lambda i: (i, 0)),
            pl.BlockSpec((None, tm, D_PLE), lambda i: (layer, i, 0)),
            _layer_spec((1, D_MODEL), layer),
            _layer_spec((D_MODEL, D_FF), layer),
            _layer_spec((D_FF, D_MODEL), layer),
            _layer_spec((1, D_MODEL), layer),
            _layer_spec((D_MODEL, D_MODEL), layer),
            _layer_spec((D_PLE, D_MODEL), layer),
        ],
        out_specs=pl.BlockSpec((tm, D_MODEL), lambda i: (i, 0)),
        out_shape=jax.ShapeDtypeStruct((t, D_MODEL), F32),
        compiler_params=_cparams(),
        name="ffn_ple",
    )(x2, p3, g_ffn, w1, w2, g_ple, w_gate, w_ple)


def _tri64():
    i = np.arange(CHUNK)
    return jnp.asarray(i[:, None] >= i[None, :], BF16)


def _ones_bd(block, n=D_BRANCH):
    i = np.arange(n) // block
    return jnp.asarray(i[:, None] == i[None, :], BF16)


def _head_masks():
    lane = np.arange(D_BRANCH) // 64
    return jnp.asarray(lane[None, :] == np.arange(4)[:, None], F32)


HG_SUB = 16
HG_TILE = 256
HG_SUB_STACK = 8 * HG_SUB + 4 * HG_SUB


def _hgrn_kernel(p_ref, vec_ref, tri_ref, onesbd_ref, bdmask_ref, hm_ref, o_ref, st_ref, bl_s, ck_s, w_s):
    seq = p_ref.shape[0]
    half = HG_SUB // 2
    loglb = vec_ref[0:1, :]
    log1mlb = vec_ref[1:2, :]
    gain = vec_ref[2:3, :]

    def pre(i, carry):
        rows = pl.ds(pl.multiple_of(i * HG_TILE, HG_TILE), HG_TILE)
        z = p_ref[rows, 256:512]
        ls = jnp.minimum(z, 0.0) - jnp.log(1.0 + jnp.exp(-jnp.abs(z)))
        y = log1mlb + ls
        lf = jnp.maximum(loglb, y) + jnp.log(1.0 + jnp.exp(-jnp.abs(loglb - y)))
        bl2 = _dot_exact_lhs(tri_ref[...], lf) * LOG2E
        bl_s[rows, :] = bl2
        ck_s[rows, :] = bl2 - (log1mlb + (ls - z)) * LOG2E
        return carry

    lax.fori_loop(0, seq // HG_TILE, pre, 0)

    row16 = lax.broadcasted_iota(jnp.int32, (HG_SUB, D_BRANCH), 0)
    row8 = lax.broadcasted_iota(jnp.int32, (half, D_BRANCH), 0)

    def chunk(c, carry):
        r0 = c * CHUNK
        subs = []
        for s in range(CHUNK // HG_SUB):
            rs = pl.multiple_of(r0 + s * HG_SUB, HG_SUB)
            rows = pl.ds(rs, HG_SUB)
            q = p_ref[rows, 0:256]
            bl = bl_s[rows, :]
            base = s * HG_SUB_STACK
            for j in range(half):
                d = jnp.where(row16 >= j, bl - ck_s[pl.ds(rs + j, 1), :], NEG_INF)
                w_s[base + j * HG_SUB:base + (j + 1) * HG_SUB, :] = (q * jnp.exp2(d)).astype(BF16)
            q_hi = q[half:, :]
            bl_hi = bl[half:, :]
            for jp in range(half // 2):
                pieces = []
                for j in (half + 2 * jp, half + 2 * jp + 1):
                    d = jnp.where(row8 >= j - half, bl_hi - ck_s[pl.ds(rs + j, 1), :], NEG_INF)
                    pieces.append(q_hi * jnp.exp2(d))
                off = base + half * HG_SUB + jp * HG_SUB
                w_s[off:off + HG_SUB, :] = jnp.concatenate(pieces, axis=0).astype(BF16)
            subs.append((rs, rows, q, bl))

        r = _dot(w_s[...], onesbd_ref[...])

        for s, (rs, rows, q, bl) in enumerate(subs):
            base = s * HG_SUB_STACK
            o_lo = jnp.zeros((half, D_BRANCH), F32)
            o_hi = jnp.zeros((half, D_BRANCH), F32)
            for j in range(HG_SUB):
                vj = p_ref[pl.ds(rs + j, 1), 512:768]
                if j < half:
                    o_lo = o_lo + r[base + j * HG_SUB:base + j * HG_SUB + half, :] * vj
                    o_hi = o_hi + r[base + j * HG_SUB + half:base + (j + 1) * HG_SUB, :] * vj
                else:
                    off = base + half * HG_SUB + (j - half) * half
                    o_hi = o_hi + r[off:off + half, :] * vj
            o_ref[rows, :] = jnp.concatenate([o_lo, o_hi], axis=0)
        return carry

    lax.fori_loop(0, seq // CHUNK, chunk, 0, unroll=2)

    st_ref[...] = jnp.zeros_like(st_ref)
    nsub = CHUNK // HG_SUB

    def carry_state(c, carry):
        rows = pl.ds(pl.multiple_of(c * CHUNK, CHUNK), CHUNK)
        q = p_ref[rows, 0:256]
        v = p_ref[rows, 512:768].astype(BF16)
        b = bl_s[rows, :]
        ck = ck_s[rows, :]
        ends = [b[(s + 1) * HG_SUB - 1:(s + 1) * HG_SUB, :] for s in range(nsub)]
        st = st_ref[...]
        o = _dot_nt((q * jnp.exp2(b)).astype(BF16), st.astype(BF16))
        k_end = jnp.exp2(ends[-1] - ck).astype(BF16)
        st_ref[...] = st * jnp.exp2(ends[-1]) + _dot_tn(v, k_end) * bdmask_ref[...]
        for s in range(nsub - 1):
            sub = slice(s * HG_SUB, (s + 1) * HG_SUB)
            after = slice((s + 1) * HG_SUB, CHUNK)
            n = CHUNK - (s + 1) * HG_SUB
            qd = q[after, :] * jnp.exp2(b[after, :] - ends[s])
            qs = jnp.concatenate([qd * hm_ref[h:h + 1, :] for h in range(HG_HEADS)], axis=0)
            ks = jnp.exp2(ends[s] - ck[sub, :]).astype(BF16)
            sc = _dot_nt(qs.astype(BF16), ks)
            pv = _dot(sc.astype(BF16), v[sub, :])
            add = pv[0:n, :] * hm_ref[0:1, :]
            for h in range(1, HG_HEADS):
                add = add + pv[h * n:(h + 1) * n, :] * hm_ref[h:h + 1, :]
            o = jnp.concatenate([o[:(s + 1) * HG_SUB, :], o[after, :] + add], axis=0)
        o_ref[rows, :] = o_ref[rows, :] + o
        return carry

    lax.fori_loop(0, seq // CHUNK, carry_state, 0, unroll=8)

    def post(i, carry):
        rows = pl.ds(pl.multiple_of(i * HG_TILE, HG_TILE), HG_TILE)
        o = o_ref[rows, :]
        g = p_ref[rows, 768:1024]
        ms = _dot_exact_rhs(o * o, onesbd_ref[...]) * (1.0 / HG_DK)
        o_ref[rows, :] = o * lax.rsqrt(ms + EPS) * gain * (g * _sigmoid(g))
        return carry

    lax.fori_loop(0, seq // HG_TILE, post, 0)


def _hgrn(proj, hg_vec, layer, batch, seq):
    bd = _ones_bd(64)
    i = np.arange(HG_TILE)
    tri = jnp.asarray((i[:, None] // CHUNK == i[None, :] // CHUNK) & (i[:, None] >= i[None, :]), BF16)
    return pl.pallas_call(
        _hgrn_kernel,
        grid=(batch,),
        in_specs=[
            pl.BlockSpec((seq, 1024), lambda b: (b, COL_HG // 1024)),
            _layer_spec((3, D_BRANCH), layer),
            _const_spec((HG_TILE, HG_TILE)), _const_spec((D_BRANCH, D_BRANCH)),
            _const_spec((D_BRANCH, D_BRANCH)), _const_spec((4, D_BRANCH)),
        ],
        out_specs=pl.BlockSpec((seq, D_BRANCH), lambda b: (b, 0)),
        out_shape=jax.ShapeDtypeStruct((batch * seq, D_BRANCH), F32),
        scratch_shapes=[
            pltpu.VMEM((D_BRANCH, D_BRANCH), F32),
            pltpu.VMEM((seq, D_BRANCH), F32),
            pltpu.VMEM((seq, D_BRANCH), F32),
            pltpu.VMEM((CHUNK // HG_SUB * HG_SUB_STACK, D_BRANCH), BF16),
        ],
        compiler_params=_cparams(),
        name="hgrn",
    )(proj, hg_vec, tri, bd, bd.astype(F32), _head_masks())


def _ssd_kernel(xbc_ref, z_ref, dt_ref, cw_ref, cb_ref, v256_ref,
                tri_ref, u_ref, gm_ref, hm_ref, o_ref, sn_ref):
    seq = xbc_ref.shape[0]
    dx_ref = v256_ref.at[0:1, :]
    ng_ref = v256_ref.at[1:2, :]
    dtb_ref = v256_ref.at[2:3, :]
    alog_ref = v256_ref.at[3:4, :]
    sn_ref[...] = jnp.zeros_like(sn_ref)
    lidx = lax.broadcasted_iota(jnp.int32, (CHUNK, D_BRANCH), 0)
    sidx = lax.broadcasted_iota(jnp.int32, (CHUNK, D_BRANCH), 1) % CHUNK
    causal = lidx >= sidx
    lane128 = lax.broadcasted_iota(jnp.int32, (1, 2 * SSD_DSTATE), 1)
    a_neg = -jnp.exp(alog_ref[...])

    def chunk(c, carry):
        r0 = pl.multiple_of(c * CHUNK, CHUNK)
        rows = pl.ds(r0, CHUNK)
        cur = xbc_ref[rows, :]
        prev = xbc_ref[pl.ds(pl.multiple_of(jnp.maximum(r0 - 8, 0), 8), 8), :]
        prev = prev * jnp.where(c > 0, 1.0, 0.0)
        ext = jnp.concatenate([prev, cur], axis=0)
        conv = cb_ref[...] + cur * cw_ref[SSD_CONV - 1:SSD_CONV, :]
        for s in range(1, SSD_CONV):
            sh = pltpu.roll(ext, s, axis=0)[8:8 + CHUNK, :]
            conv = conv + sh * cw_ref[SSD_CONV - 1 - s:SSD_CONV - s, :]
        xa = conv * _sigmoid(conv)
        xs = xa[:, 0:SSD_INNER]
        bm = xa[:, SSD_INNER:SSD_INNER + 128]
        cm = xa[:, SSD_INNER + 128:SSD_INNER + 256]

        dtx = _softplus(dt_ref[rows, :] + dtb_ref[...])
        adtx = dtx * a_neg
        cum = _dot_exact_lhs(tri_ref[...], jnp.concatenate([adtx, adtx * u_ref[...]], axis=1))
        acum = cum[:, 0:D_BRANCH]
        seg = cum[:, D_BRANCH:2 * D_BRANCH]
        lmat = jnp.exp(jnp.where(causal, seg, NEG_INF))
        xdt = xs * dtx

        bm16 = bm.astype(BF16)
        bm2 = jnp.concatenate([bm16, bm16], axis=0)
        gcat = jnp.concatenate(
            [_dot_nt(jnp.where((lane128 // SSD_DSTATE) == grp, cm, 0.0).astype(BF16), bm2)
             for grp in range(SSD_GROUPS)], axis=1)
        xbd = jnp.concatenate([xdt * hm_ref[h:h + 1, :] for h in range(SSD_HEADS)], axis=0)
        y = _dot((gcat * lmat).astype(BF16), xbd.astype(BF16))

        sn_old = sn_ref[...]
        y = y + _dot(cm.astype(BF16), sn_old.astype(BF16)) * jnp.exp(acum)
        aend = acum[CHUNK - 1:CHUNK, :]
        upd = _dot_tn(bm16, (jnp.exp(aend - acum) * xdt).astype(BF16))
        sn_ref[...] = jnp.exp(aend) * sn_old + upd * gm_ref[...]

        y = y + dx_ref[...] * xs
        zz = z_ref[rows, :]
        y = y * (zz * _sigmoid(zz))
        halves = []
        for grp in range(SSD_GROUPS):
            yh = y[:, grp * 128:(grp + 1) * 128]
            halves.append(yh * lax.rsqrt(jnp.mean(yh * yh, axis=-1, keepdims=True) + EPS))
        o_ref[rows, :] = jnp.concatenate(halves, axis=1) * ng_ref[...]
        return carry

    lax.fori_loop(0, seq // CHUNK, chunk, 0, unroll=8)


def _ssd(proj, conv_wt, conv_b, v256, layer, batch, seq):
    full = _const_spec
    li = np.arange(CHUNK)
    u_t = np.tile((li[:, None] > li[None, :]).astype(np.float32), (1, SSD_HEADS))
    gm = ((np.arange(128) // 64)[:, None] == (np.arange(256) // 128)[None, :]).astype(np.float32)
    return pl.pallas_call(
        _ssd_kernel,
        grid=(batch,),
        in_specs=[
            pl.BlockSpec((seq, SSD_XBC), lambda b: (b, COL_XBC // SSD_XBC)),
            pl.BlockSpec((seq, SSD_INNER), lambda b: (b, COL_Z // SSD_INNER)),
            pl.BlockSpec((seq, D_BRANCH), lambda b: (b, COL_DT // D_BRANCH)),
            _layer_spec((SSD_CONV, SSD_XBC), layer), _layer_spec((1, SSD_XBC), layer),
            _layer_spec((4, D_BRANCH), layer),
            full((CHUNK, CHUNK)), full((CHUNK, D_BRANCH)),
            full((128, D_BRANCH)), full((4, D_BRANCH)),
        ],
        out_specs=pl.BlockSpec((seq, D_BRANCH), lambda b: (b, 0)),
        out_shape=jax.ShapeDtypeStruct((batch * seq, D_BRANCH), F32),
        scratch_shapes=[pltpu.VMEM((2 * SSD_DSTATE, D_BRANCH), F32)],
        compiler_params=_cparams(),
        name="ssd",
    )(proj, proj, proj, conv_wt, conv_b, v256,
      _tri64(), jnp.asarray(u_t), jnp.asarray(gm), _head_masks())


S5_RT = 256


def _gelu_tanh(x):
    return 0.5 * x * (1.0 + jnp.tanh(math.sqrt(2.0 / math.pi) * (x + 0.044715 * (x * x * x))))


def _s5_kernel(ua_ref, ub_ref, kbd_ref, wst_ref, wout_ref, lam_ref, d_ref, wglu_ref, o_ref,
               hloc_s, hprev_s, ya_s, yb_s):
    u_refs = (ua_ref, ub_ref)
    y_refs = (ya_s, yb_s)
    seq = ua_ref.shape[0]
    nchunk = seq // S5_L
    hs = S5_HALF_STATE
    rowmod = lax.broadcasted_iota(jnp.int32, (S5_RT, D_BRANCH), 0) % S5_L

    def tile(i, carry):
        rows = pl.ds(pl.multiple_of(i * S5_RT, S5_RT), S5_RT)
        ut = jnp.concatenate([ua_ref[rows, :], ub_ref[rows, :]], axis=1)
        acc = _dot(ut.astype(BF16), kbd_ref[0])
        for d in range(1, S5_L):
            ud = jnp.where(rowmod >= d, pltpu.roll(ut, d, axis=0), 0.0)
            acc = acc + _dot(ud.astype(BF16), kbd_ref[d])
        ya_s[rows, :] = acc[:, 0:128]
        yb_s[rows, :] = acc[:, 128:256]
        return carry

    lax.fori_loop(0, seq // S5_RT, tile, 0)

    for half in range(2):
        acc = None
        for s in range(S5_L):
            us = u_refs[half][pl.ds(s, nchunk, stride=S5_L), :]
            term = _dot(us.astype(BF16), wst_ref[half, s])
            acc = term if acc is None else acc + term
        hloc_s[half] = acc

    lam = [(lam_ref[half, 0:1, :], lam_ref[half, 1:2, :]) for half in range(2)]

    def step(c, carry):
        out = []
        for half in range(2):
            hr, hi = carry[half]
            lr, li = lam[half]
            hprev_s[half, pl.ds(c, 1), 0:hs] = hr
            hprev_s[half, pl.ds(c, 1), hs:2 * hs] = hi
            loc = hloc_s[half, pl.ds(c, 1), :]
            out.append((lr * hr - li * hi + loc[:, 0:hs], lr * hi + li * hr + loc[:, hs:2 * hs]))
        return tuple(out)

    zero = jnp.zeros((1, hs), F32)
    lax.fori_loop(0, nchunk, step, ((zero, zero), (zero, zero)), unroll=2)

    for half in range(2):
        hp = hprev_s[half].astype(BF16)
        for s in range(S5_L):
            srows = pl.ds(s, nchunk, stride=S5_L)
            y_refs[half][srows, :] = y_refs[half][srows, :] + _dot_nt(hp, wout_ref[half, s])

    def tail(i, carry):
        rows = pl.ds(pl.multiple_of(i * S5_RT, S5_RT), S5_RT)
        ut = jnp.concatenate([ua_ref[rows, :], ub_ref[rows, :]], axis=1)
        yt = jnp.concatenate([ya_s[rows, :], yb_s[rows, :]], axis=1)
        y = _gelu_tanh(yt + d_ref[...] * ut)
        o_ref[rows, :] = y * _sigmoid(_dot(y.astype(BF16), wglu_ref[...]))
        return carry

    lax.fori_loop(0, seq // S5_RT, tail, 0)


def _s5_weights(a_re, a_im, b_re, b_im, c_re, c_im, log_dt):
    hp = lax.Precision.HIGHEST
    step = jnp.exp(log_dt)[:, None]
    mag = jnp.exp(a_re * step)
    lam_re = mag * jnp.cos(a_im * step)
    lam_im = mag * jnp.sin(a_im * step)
    den = a_re * a_re + a_im * a_im
    num_re = lam_re - 1.0
    coef_re = (num_re * a_re + lam_im * a_im) / den
    coef_im = (lam_im * a_re - num_re * a_im) / den
    bb_re = coef_re[..., None] * b_re - coef_im[..., None] * b_im
    bb_im = coef_re[..., None] * b_im + coef_im[..., None] * b_re

    pw_re = [jnp.ones_like(lam_re)]
    pw_im = [jnp.zeros_like(lam_im)]
    for _ in range(S5_L):
        pr, pi = pw_re[-1], pw_im[-1]
        pw_re.append(pr * lam_re - pi * lam_im)
        pw_im.append(pr * lam_im + pi * lam_re)
    rev_re = jnp.stack(pw_re[S5_L - 1::-1])
    rev_im = jnp.stack(pw_im[S5_L - 1::-1])
    pw_re = jnp.stack(pw_re)
    pw_im = jnp.stack(pw_im)

    bbt_re = jnp.swapaxes(bb_re, 1, 2)[None]
    bbt_im = jnp.swapaxes(bb_im, 1, 2)[None]
    cp_re = c_re[None] * pw_re[:, :, None, :] - c_im[None] * pw_im[:, :, None, :]
    cp_im = c_re[None] * pw_im[:, :, None, :] + c_im[None] * pw_re[:, :, None, :]
    kd = jnp.einsum('gjq,dgiq->dgji', jnp.concatenate([bbt_re[0], bbt_im[0]], axis=-1),
                    jnp.concatenate([cp_re[:S5_L], -cp_im[:S5_L]], axis=-1), precision=hp)

    def halves(t):
        return jnp.moveaxis(t.reshape((S5_L, 2, S5_HALF_G) + t.shape[2:]), 1, 0)

    rr, ri_ = rev_re[:, :, None, :], rev_im[:, :, None, :]
    wbt = jnp.stack([halves(rr * bbt_re - ri_ * bbt_im),
                     halves(rr * bbt_im + ri_ * bbt_re)], axis=4)
    cpn = jnp.stack([halves(cp_re[1:S5_L + 1]), halves(-cp_im[1:S5_L + 1])], axis=4)
    compact = jnp.stack([wbt, cpn]).reshape(2, 2, S5_L, S5_HALF_G, S5_GROUP_CH, 2 * S5_STATE)
    lam_l = jnp.stack([pw_re[S5_L].reshape(2, S5_HALF_STATE),
                       pw_im[S5_L].reshape(2, S5_HALF_STATE)], axis=1)
    return kd, compact, lam_l


def _s5_expand_kernel(c_ref, o_ref):
    o_ref[...] = jnp.zeros(o_ref.shape, BF16)
    for s in range(S5_L):
        for g in range(S5_HALF_G):
            blk = c_ref[s, g]
            for ri in range(2):
                lo = ri * S5_HALF_STATE + g * S5_STATE
                o_ref[s, g * S5_GROUP_CH:(g + 1) * S5_GROUP_CH, lo:lo + S5_STATE] = (
                    blk[:, ri * S5_STATE:(ri + 1) * S5_STATE].astype(BF16))


def _s5_kbd_kernel(k_ref, o_ref):
    o_ref[...] = jnp.zeros(o_ref.shape, BF16)
    for d in range(S5_L):
        for g in range(S5_GROUPS):
            lo = g * S5_GROUP_CH
            o_ref[d, lo:lo + S5_GROUP_CH, lo:lo + S5_GROUP_CH] = k_ref[d, g].astype(BF16)


def _s5_kbd(kd):
    depth = kd.shape[0]
    return pl.pallas_call(
        _s5_kbd_kernel,
        grid=(depth,),
        in_specs=[pl.BlockSpec((None, S5_L, S5_GROUPS, S5_GROUP_CH, S5_GROUP_CH), lambda l: (l, 0, 0, 0, 0))],
        out_specs=pl.BlockSpec((None, S5_L, D_BRANCH, D_BRANCH), lambda l: (l, 0, 0, 0)),
        out_shape=jax.ShapeDtypeStruct((depth, S5_L, D_BRANCH, D_BRANCH), BF16),
        compiler_params=_cparams(),
        name="s5_kbd",
    )(kd)


def _s5_expand(compact):
    depth = compact.shape[0]
    blk_in = (None, None, None, S5_L, S5_HALF_G, S5_GROUP_CH, 2 * S5_STATE)
    blk_out = (None, None, None, S5_L, S5_HALF_G * S5_GROUP_CH, 2 * S5_HALF_STATE)
    idx = lambda l, w, h: (l, w, h, 0, 0, 0, 0)
    return pl.pallas_call(
        _s5_expand_kernel,
        grid=(depth, 2, 2),
        in_specs=[pl.BlockSpec(blk_in, idx)],
        out_specs=pl.BlockSpec(blk_out, lambda l, w, h: (l, w, h, 0, 0, 0)),
        out_shape=jax.ShapeDtypeStruct((depth, 2, 2, S5_L, 128, 2 * S5_HALF_STATE), BF16),
        compiler_params=_cparams(3),
        name="s5_expand",
    )(compact)


def _s5(proj, kbd, wmats, lam_l, d_skip, w_glu, layer, batch, seq):
    nchunk = seq // S5_L
    wspec = lambda which: pl.BlockSpec(
        (None, None, 2, S5_L, 128, 2 * S5_HALF_STATE), lambda b: (layer, which, 0, 0, 0, 0),
        pipeline_mode=pl.Buffered(1))
    return pl.pallas_call(
        _s5_kernel,
        grid=(batch,),
        in_specs=[
            pl.BlockSpec((seq, 128), lambda b: (b, COL_U // 128)),
            pl.BlockSpec((seq, 128), lambda b: (b, COL_U // 128 + 1)),
            _layer_spec((S5_L, D_BRANCH, D_BRANCH), layer),
            wspec(0), wspec(1),
            _layer_spec((2, 2, S5_HALF_STATE), layer),
            _layer_spec((1, D_BRANCH), layer),
            _layer_spec((D_BRANCH, D_BRANCH), layer),
        ],
        out_specs=pl.BlockSpec((seq, D_BRANCH), lambda b: (b, 0)),
        out_shape=jax.ShapeDtypeStruct((batch * seq, D_BRANCH), F32),
        scratch_shapes=[
            pltpu.VMEM((2, nchunk, 2 * S5_HALF_STATE), F32),
            pltpu.VMEM((2, nchunk, 2 * S5_HALF_STATE), F32),
            pltpu.VMEM((seq, 128), F32),
            pltpu.VMEM((seq, 128), F32),
        ],
        compiler_params=_cparams(),
        name="s5",
    )(proj, proj, kbd, wmats, wmats, lam_l, d_skip, w_glu)


def _attn_kernel(q_ref, k_ref, v_ref, gains_ref, bias_ref, onesbd_ref, hm_ref, o_ref,
                 kp_s, vp_s, qn_s):
    seq = q_ref.shape[0]
    qg_ref = gains_ref.at[0:1, :]
    kg_ref = gains_ref.at[1:2, :]
    kp_s[0:ATT_PAD, :] = jnp.zeros((ATT_PAD, D_BRANCH), BF16)
    vp_s[0:ATT_PAD, :] = jnp.zeros((ATT_PAD, D_BRANCH), BF16)
    rt = 256
    scale = ATT_HEADDIM ** -0.5

    def prep(i, carry):
        rows = pl.ds(pl.multiple_of(i * rt, rt), rt)
        k = k_ref[rows, :]
        ms = _dot_exact_rhs(k * k, onesbd_ref[...]) * (1.0 / ATT_HEADDIM)
        kn = k * lax.rsqrt(ms + EPS) * kg_ref[...]
        prow = pl.ds(pl.multiple_of(ATT_PAD + i * rt, 64), rt)
        kp_s[prow, :] = kn.astype(BF16)
        vp_s[prow, :] = v_ref[rows, :].astype(BF16)
        q = q_ref[rows, :]
        ms = _dot_exact_rhs(q * q, onesbd_ref[...]) * (1.0 / ATT_HEADDIM)
        qn_s[rows, :] = q * lax.rsqrt(ms + EPS) * (qg_ref[...] * scale)
        return carry

    lax.fori_loop(0, seq // rt, prep, 0)

    pidx = lax.broadcasted_iota(jnp.int32, (ATT_HEADS * CHUNK, ATT_BAND), 1)

    def chunk(c, carry):
        r0 = pl.multiple_of(c * CHUNK, CHUNK)
        qn = qn_s[pl.ds(r0, CHUNK), :]
        qs = jnp.concatenate([qn * hm_ref[h:h + 1, :] for h in range(ATT_HEADS)], axis=0).astype(BF16)
        kb = kp_s[pl.ds(r0, ATT_BAND), :]
        vb = vp_s[pl.ds(r0, ATT_BAND), :]
        s = _dot_nt(qs, kb)
        first_valid = jnp.maximum(ATT_LEFT_CHUNKS + 1 - c, 1) * CHUNK
        s = jnp.where(pidx >= first_valid, s + bias_ref[...], NEG_INF)
        m = jnp.max(s, axis=-1, keepdims=True)
        e = jnp.exp(s - m)
        l = jnp.sum(e, axis=-1, keepdims=True)
        pv = _dot(e.astype(BF16), vb) / l
        out = pv[0:CHUNK, :] * hm_ref[0:1, :]
        for h in range(1, ATT_HEADS):
            out = out + pv[h * CHUNK:(h + 1) * CHUNK, :] * hm_ref[h:h + 1, :]
        o_ref[pl.ds(r0, CHUNK), :] = out
        return carry

    lax.fori_loop(0, seq // CHUNK, chunk, 0, unroll=8)


def _attn_bias(rel_bias):
    n = CHUNK + ATT_BAND + 1
    j = np.arange(n)
    delta = np.where(j < ATT_BAND + 2, -j, n - j)
    rel = np.clip(delta + (ATT_LEFT_CHUNKS + 1) * CHUNK, -ATT_MAX_REL, ATT_MAX_REL) + ATT_MAX_REL
    d = rel_bias[:, rel]
    flat = jnp.tile(d, (1, CHUNK))[:, :CHUNK * (n - 1)]
    bias = flat.reshape(ATT_HEADS, CHUNK, n - 1)[:, :, :ATT_BAND]
    return bias.reshape(ATT_HEADS * CHUNK, ATT_BAND)


def _attn(proj, gains, bias, layer, batch, seq):
    full = _const_spec
    colblk = lambda j: pl.BlockSpec((seq, D_BRANCH), lambda b: (b, COL_QKV // D_BRANCH + j))
    return pl.pallas_call(
        _attn_kernel,
        grid=(batch,),
        in_specs=[
            colblk(0), colblk(1), colblk(2),
            _layer_spec((2, D_BRANCH), layer),
            _layer_spec((ATT_HEADS * CHUNK, ATT_BAND), layer),
            full((D_BRANCH, D_BRANCH)), full((4, D_BRANCH)),
        ],
        out_specs=pl.BlockSpec((seq, D_BRANCH), lambda b: (b, 0)),
        out_shape=jax.ShapeDtypeStruct((batch * seq, D_BRANCH), F32),
        scratch_shapes=[
            pltpu.VMEM((ATT_PAD + seq, D_BRANCH), BF16),
            pltpu.VMEM((ATT_PAD + seq, D_BRANCH), BF16),
            pltpu.VMEM((seq, D_BRANCH), F32),
        ],
        compiler_params=_cparams(),
        name="attn",
    )(proj, proj, proj, gains, bias, _ones_bd(64), _head_masks())


def kernel(x, p, norm_mix, w_in, hg_lb_logits, hg_o_norm, ssd_conv_w, ssd_conv_b, ssd_dt_bias, ssd_A_log, ssd_D, ssd_norm, s5_A_re, s5_A_im, s5_B_re, s5_B_im, s5_C_re, s5_C_im, s5_D, s5_log_dt, s5_w_glu, att_q_norm, att_k_norm, att_rel_bias, w_branch, w_out, norm_ffn, w_ff1, w_ff2, w_ple, norm_ple, w_ple_gate):
    batch, seq, _ = x.shape
    depth = w_in.shape[0]
    t = batch * seq
    row3 = lambda a: a.reshape(depth, 1, -1)

    lb_all = jnp.cumsum(jax.nn.softmax(hg_lb_logits.astype(F32), axis=0), axis=0)
    lb_all = lb_all - lb_all[0:1]
    hg_vec = jnp.stack([jnp.log(lb_all), jnp.log1p(-lb_all), hg_o_norm], axis=1)
    spread = lambda v: jnp.repeat(v, SSD_HEADDIM, axis=1)
    ssd_v256 = jnp.stack([spread(ssd_D), ssd_norm, spread(ssd_dt_bias), spread(ssd_A_log)], axis=1)
    conv_wt = jnp.swapaxes(ssd_conv_w, 1, 2)
    s5_kd, s5_compact, lam_l = jax.vmap(_s5_weights)(s5_A_re, s5_A_im, s5_B_re, s5_B_im, s5_C_re,
                                                     s5_C_im, s5_log_dt)
    s5_mats = _s5_expand(s5_compact)
    kbd = _s5_kbd(s5_kd)
    att_gains = jnp.stack([jnp.tile(att_q_norm, (1, ATT_HEADS)), jnp.tile(att_k_norm, (1, ATT_HEADS))],
                          axis=1)
    att_bias = jax.vmap(_attn_bias)(att_rel_bias)
    w_mix, w_gate = _repack(w_in)
    w_branch16, w_out16, w_glu16 = w_branch.astype(BF16), w_out.astype(BF16), s5_w_glu.astype(BF16)
    w_ff1_16, w_ff2_16 = w_ff1.astype(BF16), w_ff2.astype(BF16)
    w_pg16, w_ple16 = w_ple_gate.astype(BF16), w_ple.astype(BF16)
    g_mix, g_ffn, g_ple = row3(norm_mix), row3(norm_ffn), row3(norm_ple)

    x2 = x.reshape(t, D_MODEL)
    p3 = p.reshape(depth, t, D_PLE)
    for i in range(depth):
        proj = _inproj(x2, g_mix, w_mix, i)
        y_a = _hgrn(proj, hg_vec, i, batch, seq)
        y_b = _ssd(proj, conv_wt, row3(ssd_conv_b), ssd_v256, i, batch, seq)
        y_c = _s5(proj, kbd, s5_mats, lam_l, row3(s5_D), w_glu16, i, batch, seq)
        y_d = _attn(proj, att_gains, att_bias, i, batch, seq)
        x2 = _merge(x2, y_a, y_b, y_c, y_d, g_mix, w_gate, w_branch16, w_out16, i)
        x2 = _ffn_ple(x2, p3, g_ffn, w_ff1_16, w_ff2_16, g_ple, w_pg16, w_ple16, i)
    return x2.reshape(batch, seq, D_MODEL)
```

```python
import math

import jax
import jax.numpy as jnp
import numpy as np
from jax import lax
from jax.experimental import pallas as pl
from jax.experimental.pallas import tpu as pltpu

F32 = jnp.float32
BF16 = jnp.bfloat16

D_MODEL = 1024
CHUNK = 64
D_PLE = 256
N_BRANCH = 4
D_BRANCH = 256
D_FF = 4096
EPS = 1e-6
NEG_INF = -1e30
LOG2E = math.log2(math.e)

HG_HEADS = 4
HG_DK = 64
SSD_HEADS = 4
SSD_HEADDIM = 64
SSD_GROUPS = 2
SSD_DSTATE = 64
SSD_CONV = 4
SSD_INNER = 256
SSD_XBC = 512
S5_GROUP_CH = 16
S5_GROUPS = 16
S5_STATE = 64
ATT_HEADS = 4
ATT_HEADDIM = 64
ATT_LEFT_CHUNKS = 8
ATT_MAX_REL = 128

COL_HG = 0
COL_XBC = 1024
COL_Z = 1536
COL_U = 1792
COL_QKV = 2048
COL_DT = 2816
N_MIX = 3072
DT_PAD = 128

W_HG, W_Z, W_XBC, W_DT, W_U, W_QKV, W_GATE = 0, 1024, 1280, 1792, 1796, 2052, 2820
N_IN = 6916

S5_L = 8
S5_HALF_G = 8
S5_HALF_STATE = S5_HALF_G * S5_STATE

ATT_BAND = (ATT_LEFT_CHUNKS + 2) * CHUNK
ATT_PAD = (ATT_LEFT_CHUNKS + 1) * CHUNK

VMEM_LIMIT = 56 * 1024 * 1024


def _cparams(n_axes=1):
    return pltpu.CompilerParams(
        dimension_semantics=("arbitrary",) * n_axes, vmem_limit_bytes=VMEM_LIMIT)


def _layer_spec(shape, layer):
    return pl.BlockSpec((None,) + tuple(shape), lambda *_: (layer,) + (0,) * len(shape),
                        pipeline_mode=pl.Buffered(1))


def _const_spec(shape):
    return pl.BlockSpec(tuple(shape), lambda *_: (0,) * len(shape), pipeline_mode=pl.Buffered(1))


def _dot(a, b):
    return jnp.dot(a, b, preferred_element_type=F32)


def _dot_nt(a, b):
    return lax.dot_general(a, b, (((1,), (1,)), ((), ())), preferred_element_type=F32)


def _dot_tn(a, b):
    return lax.dot_general(a, b, (((0,), (0,)), ((), ())), preferred_element_type=F32)


def _split3(a):
    a1 = a.astype(BF16)
    r1 = a - a1.astype(F32)
    a2 = r1.astype(BF16)
    r2 = r1 - a2.astype(F32)
    return a1, a2, r2.astype(BF16)


def _dot_exact_rhs(a, m):
    a1, a2, a3 = _split3(a)
    return _dot(a1, m) + _dot(a2, m) + _dot(a3, m)


def _dot_exact_lhs(m, a):
    a1, a2, a3 = _split3(a)
    return _dot(m, a1) + _dot(m, a2) + _dot(m, a3)


def _sigmoid(x):
    return 1.0 / (1.0 + jnp.exp(-x))


def _softplus(x):
    return jnp.maximum(x, 0.0) + jnp.log(1.0 + jnp.exp(-jnp.abs(x)))


def _rms_rows(x, g):
    return x * lax.rsqrt(jnp.mean(x * x, axis=-1, keepdims=True) + EPS) * g


REPACK_TN = 256
MIX_TILE_SRC = (W_HG, W_HG + 256, W_HG + 512, W_HG + 768, W_XBC, W_XBC + 256, W_Z, W_U,
                W_QKV, W_QKV + 256, W_QKV + 512, W_DT)
DT_TILE = COL_DT // REPACK_TN


def _repack_mix_kernel(src_ref, w_ref, sel_ref, o_ref):
    del src_ref
    j = pl.program_id(0)
    for l in range(o_ref.shape[0]):
        wt = w_ref[:, l, :].T.astype(BF16)

        @pl.when(j != DT_TILE)
        def _():
            o_ref[l] = wt

        @pl.when(j == DT_TILE)
        def _():
            o_ref[l] = _dot(wt, sel_ref[...]).astype(BF16)


def _repack_gate_kernel(w_ref, o_ref):
    for l in range(o_ref.shape[0]):
        o_ref[l] = w_ref[:, l, :].T.astype(BF16)


def _repack(w_in):
    depth = w_in.shape[0]
    w_t = jnp.transpose(w_in, (2, 0, 1))
    sel = np.zeros((REPACK_TN, D_BRANCH), np.float32)
    for h in range(SSD_HEADS):
        sel[h, h * 64:(h + 1) * 64] = 1.0
    w_mix = pl.pallas_call(
        _repack_mix_kernel,
        grid_spec=pltpu.PrefetchScalarGridSpec(
            num_scalar_prefetch=1,
            grid=(N_MIX // REPACK_TN,),
            in_specs=[pl.BlockSpec((pl.Element(REPACK_TN), pl.Element(depth), pl.Element(D_MODEL)),
                                   lambda j, src: (src[j], 0, 0)),
                      pl.BlockSpec((REPACK_TN, D_BRANCH), lambda j, src: (0, 0))],
            out_specs=pl.BlockSpec((depth, D_MODEL, REPACK_TN), lambda j, src: (0, 0, j)),
        ),
        out_shape=jax.ShapeDtypeStruct((depth, D_MODEL, N_MIX), BF16),
        compiler_params=_cparams(),
        name="repack_mix",
    )(jnp.asarray(MIX_TILE_SRC, jnp.int32), w_t, jnp.asarray(sel, BF16))
    n_gate = N_BRANCH * D_MODEL
    w_gate = pl.pallas_call(
        _repack_gate_kernel,
        grid=(n_gate // REPACK_TN,),
        in_specs=[pl.BlockSpec((pl.Element(REPACK_TN), pl.Element(depth), pl.Element(D_MODEL)),
                               lambda j: (W_GATE + j * REPACK_TN, 0, 0))],
        out_specs=pl.BlockSpec((depth, D_MODEL, REPACK_TN), lambda j: (0, 0, j)),
        out_shape=jax.ShapeDtypeStruct((depth, D_MODEL, n_gate), BF16),
        compiler_params=_cparams(),
        name="repack_gate",
    )(w_t)
    return w_mix, w_gate


def _inproj_kernel(x_ref, g_ref, w_ref, o_ref):
    h = _rms_rows(x_ref[...], g_ref[...])
    o_ref[...] = _dot(h.astype(BF16), w_ref[...])


def _inproj(x2, g, w_mix, layer, tm=1024):
    t = x2.shape[0]
    return pl.pallas_call(
        _inproj_kernel,
        grid=(t // tm,),
        in_specs=[
            pl.BlockSpec((tm, D_MODEL), lambda i: (i, 0)),
            _layer_spec((1, D_MODEL), layer),
            _layer_spec((D_MODEL, N_MIX), layer),
        ],
        out_specs=pl.BlockSpec((tm, N_MIX), lambda i: (i, 0)),
        out_shape=jax.ShapeDtypeStruct((t, N_MIX), F32),
        compiler_params=_cparams(),
        name="inproj",
    )(x2, g, w_mix)


def _merge_kernel(x_ref, ya_ref, yb_ref, yc_ref, yd_ref, g_ref, wg_ref, wb_ref, wo_ref, o_ref):
    x = x_ref[...]
    h = _rms_rows(x, g_ref[...]).astype(BF16)
    merged = None
    for m, y_ref in enumerate((ya_ref, yb_ref, yc_ref, yd_ref)):
        gate = _sigmoid(_dot(h, wg_ref[:, m * D_MODEL:(m + 1) * D_MODEL]))
        term = gate * _dot(y_ref[...].astype(BF16), wb_ref[m])
        merged = term if merged is None else merged + term
    o_ref[...] = x + _dot(merged.astype(BF16), wo_ref[...])


def _merge(x2, ya, yb, yc, yd, g, w_gate, w_branch, w_out, layer, tm=512):
    t = x2.shape[0]
    row = lambda i: (i, 0)
    return pl.pallas_call(
        _merge_kernel,
        grid=(t // tm,),
        in_specs=[
            pl.BlockSpec((tm, D_MODEL), row),
            pl.BlockSpec((tm, D_BRANCH), row),
            pl.BlockSpec((tm, D_BRANCH), row),
            pl.BlockSpec((tm, D_BRANCH), row),
            pl.BlockSpec((tm, D_BRANCH), row),
            _layer_spec((1, D_MODEL), layer),
            _layer_spec((D_MODEL, N_BRANCH * D_MODEL), layer),
            _layer_spec((N_BRANCH, D_BRANCH, D_MODEL), layer),
            _layer_spec((D_MODEL, D_MODEL), layer),
        ],
        out_specs=pl.BlockSpec((tm, D_MODEL), row),
        out_shape=jax.ShapeDtypeStruct((t, D_MODEL), F32),
        compiler_params=_cparams(),
        name="merge",
    )(x2, ya, yb, yc, yd, g, w_gate, w_branch, w_out)


def _ffn_ple_kernel(x_ref, p_ref, g1_ref, w1_ref, w2_ref, g2_ref, wg_ref, wp_ref, o_ref):
    x = x_ref[...]
    h = _rms_rows(x, g1_ref[...]).astype(BF16)
    acc = x
    tk = 1024
    for j in range(D_FF // tk):
        a = jnp.maximum(_dot(h, w1_ref[:, j * tk:(j + 1) * tk]), 0.0)
        acc = acc + _dot((a * a).astype(BF16), w2_ref[j * tk:(j + 1) * tk, :])
    h2 = _rms_rows(acc, g2_ref[...]).astype(BF16)
    gate = _sigmoid(_dot(h2, wg_ref[...]))
    o_ref[...] = acc + _dot(p_ref[...].astype(BF16), wp_ref[...]) * gate


def _ffn_ple(x2, p3, g_ffn, w1, w2, g_ple, w_gate, w_ple, layer, tm=1024):
    t = x2.shape[0]
    return pl.pallas_call(
        _ffn_ple_kernel,
        grid=(t // tm,),
        in_specs=[
            pl.BlockSpec((tm, D_MODEL), lambda i: (i, 0)),
            pl.BlockSpec((None, tm, D_PLE), lambda i: (layer, i, 0)),
            _layer_spec((1, D_MODEL), layer),
            _layer_spec((D_MODEL, D_FF), layer),
            _layer_spec((D_FF, D_MODEL), layer),
            _layer_spec((1, D_MODEL), layer),
            _layer_spec((D_MODEL, D_MODEL), layer),
            _layer_spec((D_PLE, D_MODEL), layer),
        ],
        out_specs=pl.BlockSpec((tm, D_MODEL), lambda i: (i, 0)),
        out_shape=jax.ShapeDtypeStruct((t, D_MODEL), F32),
        compiler_params=_cparams(),
        name="ffn_ple",
    )(x2, p3, g_ffn, w1, w2, g_ple, w_gate, w_ple)


def _tri64():
    i = np.arange(CHUNK)
    return jnp.asarray(i[:, None] >= i[None, :], BF16)


def _ones_bd(block, n=D_BRANCH):
    i = np.arange(n) // block
    return jnp.asarray(i[:, None] == i[None, :], BF16)


def _head_masks():
    lane = np.arange(D_BRANCH) // 64
    return jnp.asarray(lane[None, :] == np.arange(4)[:, None], F32)


HG_SUB = 16
HG_TILE = 256
HG_SUB_STACK = 8 * HG_SUB + 4 * HG_SUB


def _hgrn_kernel(p_ref, vec_ref, tri_ref, onesbd_ref, bdmask_ref, hm_ref, o_ref, st_ref, bl_s, ck_s, w_s):
    seq = p_ref.shape[0]
    half = HG_SUB // 2
    loglb = vec_ref[0:1, :]
    log1mlb = vec_ref[1:2, :]
    gain = vec_ref[2:3, :]

    def pre(i, carry):
        rows = pl.ds(pl.multiple_of(i * HG_TILE, HG_TILE), HG_TILE)
        z = p_ref[rows, 256:512]
        ls = jnp.minimum(z, 0.0) - jnp.log(1.0 + jnp.exp(-jnp.abs(z)))
        y = log1mlb + ls
        lf = jnp.maximum(loglb, y) + jnp.log(1.0 + jnp.exp(-jnp.abs(loglb - y)))
        bl2 = _dot_exact_lhs(tri_ref[...], lf) * LOG2E
        bl_s[rows, :] = bl2
        ck_s[rows, :] = bl2 - (log1mlb + (ls - z)) * LOG2E
        return carry

    lax.fori_loop(0, seq // HG_TILE, pre, 0)

    row16 = lax.broadcasted_iota(jnp.int32, (HG_SUB, D_BRANCH), 0)
    row8 = lax.broadcasted_iota(jnp.int32, (half, D_BRANCH), 0)

    def chunk(c, carry):
        r0 = c * CHUNK
        subs = []
        for s in range(CHUNK // HG_SUB):
            rs = pl.multiple_of(r0 + s * HG_SUB, HG_SUB)
            rows = pl.ds(rs, HG_SUB)
            q = p_ref[rows, 0:256]
            bl = bl_s[rows, :]
            base = s * HG_SUB_STACK
            for j in range(half):
                d = jnp.where(row16 >= j, bl - ck_s[pl.ds(rs + j, 1), :], NEG_INF)
                w_s[base + j * HG_SUB:base + (j + 1) * HG_SUB, :] = (q * jnp.exp2(d)).astype(BF16)
            q_hi = q[half:, :]
            bl_hi = bl[half:, :]
            for jp in range(half // 2):
                pieces = []
                for j in (half + 2 * jp, half + 2 * jp + 1):
                    d = jnp.where(row8 >= j - half, bl_hi - ck_s[pl.ds(rs + j, 1), :], NEG_INF)
                    pieces.append(q_hi * jnp.exp2(d))
                off = base + half * HG_SUB + jp * HG_SUB
                w_s[off:off + HG_SUB, :] = jnp.concatenate(pieces, axis=0).astype(BF16)
            subs.append((rs, rows, q, bl))

        r = _dot(w_s[...], onesbd_ref[...])

        for s, (rs, rows, q, bl) in enumerate(subs):
            base = s * HG_SUB_STACK
            o_lo = jnp.zeros((half, D_BRANCH), F32)
            o_hi = jnp.zeros((half, D_BRANCH), F32)
            for j in range(HG_SUB):
                vj = p_ref[pl.ds(rs + j, 1), 512:768]
                if j < half:
                    o_lo = o_lo + r[base + j * HG_SUB:base + j * HG_SUB + half, :] * vj
                    o_hi = o_hi + r[base + j * HG_SUB + half:base + (j + 1) * HG_SUB, :] * vj
                else:
                    off = base + half * HG_SUB + (j - half) * half
                    o_hi = o_hi + r[off:off + half, :] * vj
            o_ref[rows, :] = jnp.concatenate([o_lo, o_hi], axis=0)
        return carry

    lax.fori_loop(0, seq // CHUNK, chunk, 0, unroll=2)

    st_ref[...] = jnp.zeros_like(st_ref)
    nsub = CHUNK // HG_SUB

    def carry_state(c, carry):
        rows = pl.ds(pl.multiple_of(c * CHUNK, CHUNK), CHUNK)
        q = p_ref[rows, 0:256]
        v = p_ref[rows, 512:768].astype(BF16)
        b = bl_s[rows, :]
        ck = ck_s[rows, :]
        ends = [b[(s + 1) * HG_SUB - 1:(s + 1) * HG_SUB, :] for s in range(nsub)]
        st = st_ref[...]
        o = _dot_nt((q * jnp.exp2(b)).astype(BF16), st.astype(BF16))
        k_end = jnp.exp2(ends[-1] - ck).astype(BF16)
        st_ref[...] = st * jnp.exp2(ends[-1]) + _dot_tn(v, k_end) * bdmask_ref[...]
        for s in range(nsub - 1):
            sub = slice(s * HG_SUB, (s + 1) * HG_SUB)
            after = slice((s + 1) * HG_SUB, CHUNK)
            n = CHUNK - (s + 1) * HG_SUB
            qd = q[after, :] * jnp.exp2(b[after, :] - ends[s])
            qs = jnp.concatenate([qd * hm_ref[h:h + 1, :] for h in range(HG_HEADS)], axis=0)
            ks = jnp.exp2(ends[s] - ck[sub, :]).astype(BF16)
            sc = _dot_nt(qs.astype(BF16), ks)
            pv = _dot(sc.astype(BF16), v[sub, :])
            add = pv[0:n, :] * hm_ref[0:1, :]
            for h in range(1, HG_HEADS):
                add = add + pv[h * n:(h + 1) * n, :] * hm_ref[h:h + 1, :]
            o = jnp.concatenate([o[:(s + 1) * HG_SUB, :], o[after, :] + add], axis=0)
        o_ref[rows, :] = o_ref[rows, :] + o
        return carry

    lax.fori_loop(0, seq // CHUNK, carry_state, 0, unroll=8)

    def post(i, carry):
        rows = pl.ds(pl.multiple_of(i * HG_TILE, HG_TILE), HG_TILE)
        o = o_ref[rows, :]
        g = p_ref[rows, 768:1024]
        ms = _dot_exact_rhs(o * o, onesbd_ref[...]) * (1.0 / HG_DK)
        o_ref[rows, :] = o * lax.rsqrt(ms + EPS) * gain * (g * _sigmoid(g))
        return carry

    lax.fori_loop(0, seq // HG_TILE, post, 0)


def _hgrn(proj, hg_vec, layer, batch, seq):
    bd = _ones_bd(64)
    i = np.arange(HG_TILE)
    tri = jnp.asarray((i[:, None] // CHUNK == i[None, :] // CHUNK) & (i[:, None] >= i[None, :]), BF16)
    return pl.pallas_call(
        _hgrn_kernel,
        grid=(batch,),
        in_specs=[
            pl.BlockSpec((seq, 1024), lambda b: (b, COL_HG // 1024)),
            _layer_spec((3, D_BRANCH), layer),
            _const_spec((HG_TILE, HG_TILE)), _const_spec((D_BRANCH, D_BRANCH)),
            _const_spec((D_BRANCH, D_BRANCH)), _const_spec((4, D_BRANCH)),
        ],
        out_specs=pl.BlockSpec((seq, D_BRANCH), lambda b: (b, 0)),
        out_shape=jax.ShapeDtypeStruct((batch * seq, D_BRANCH), F32),
        scratch_shapes=[
            pltpu.VMEM((D_BRANCH, D_BRANCH), F32),
            pltpu.VMEM((seq, D_BRANCH), F32),
            pltpu.VMEM((seq, D_BRANCH), F32),
            pltpu.VMEM((CHUNK // HG_SUB * HG_SUB_STACK, D_BRANCH), BF16),
        ],
        compiler_params=_cparams(),
        name="hgrn",
    )(proj, hg_vec, tri, bd, bd.astype(F32), _head_masks())


def _ssd_kernel(xbc_ref, z_ref, dt_ref, cw_ref, cb_ref, v256_ref,
                tri_ref, u_ref, gm_ref, hm_ref, o_ref, sn_ref):
    seq = xbc_ref.shape[0]
    dx_ref = v256_ref.at[0:1, :]
    ng_ref = v256_ref.at[1:2, :]
    dtb_ref = v256_ref.at[2:3, :]
    alog_ref = v256_ref.at[3:4, :]
    sn_ref[...] = jnp.zeros_like(sn_ref)
    lidx = lax.broadcasted_iota(jnp.int32, (CHUNK, D_BRANCH), 0)
    sidx = lax.broadcasted_iota(jnp.int32, (CHUNK, D_BRANCH), 1) % CHUNK
    causal = lidx >= sidx
    lane128 = lax.broadcasted_iota(jnp.int32, (1, 2 * SSD_DSTATE), 1)
    a_neg = -jnp.exp(alog_ref[...])

    def chunk(c, carry):
        r0 = pl.multiple_of(c * CHUNK, CHUNK)
        rows = pl.ds(r0, CHUNK)
        cur = xbc_ref[rows, :]
        prev = xbc_ref[pl.ds(pl.multiple_of(jnp.maximum(r0 - 8, 0), 8), 8), :]
        prev = prev * jnp.where(c > 0, 1.0, 0.0)
        ext = jnp.concatenate([prev, cur], axis=0)
        conv = cb_ref[...] + cur * cw_ref[SSD_CONV - 1:SSD_CONV, :]
        for s in range(1, SSD_CONV):
            sh = pltpu.roll(ext, s, axis=0)[8:8 + CHUNK, :]
            conv = conv + sh * cw_ref[SSD_CONV - 1 - s:SSD_CONV - s, :]
        xa = conv * _sigmoid(conv)
        xs = xa[:, 0:SSD_INNER]
        bm = xa[:, SSD_INNER:SSD_INNER + 128]
        cm = xa[:, SSD_INNER + 128:SSD_INNER + 256]

        dtx = _softplus(dt_ref[rows, :] + dtb_ref[...])
        adtx = dtx * a_neg
        cum = _dot_exact_lhs(tri_ref[...], jnp.concatenate([adtx, adtx * u_ref[...]], axis=1))
        acum = cum[:, 0:D_BRANCH]
        seg = cum[:, D_BRANCH:2 * D_BRANCH]
        lmat = jnp.exp(jnp.where(causal, seg, NEG_INF))
        xdt = xs * dtx

        bm16 = bm.astype(BF16)
        bm2 = jnp.concatenate([bm16, bm16], axis=0)
        gcat = jnp.concatenate(
            [_dot_nt(jnp.where((lane128 // SSD_DSTATE) == grp, cm, 0.0).astype(BF16), bm2)
             for grp in range(SSD_GROUPS)], axis=1)
        xbd = jnp.concatenate([xdt * hm_ref[h:h + 1, :] for h in range(SSD_HEADS)], axis=0)
        y = _dot((gcat * lmat).astype(BF16), xbd.astype(BF16))

        sn_old = sn_ref[...]
        y = y + _dot(cm.astype(BF16), sn_old.astype(BF16)) * jnp.exp(acum)
        aend = acum[CHUNK - 1:CHUNK, :]
        upd = _dot_tn(bm16, (jnp.exp(aend - acum) * xdt).astype(BF16))
        sn_ref[...] = jnp.exp(aend) * sn_old + upd * gm_ref[...]

        y = y + dx_ref[...] * xs
        zz = z_ref[rows, :]
        y = y * (zz * _sigmoid(zz))
        halves = []
        for grp in range(SSD_GROUPS):
            yh = y[:, grp * 128:(grp + 1) * 128]
            halves.append(yh * lax.rsqrt(jnp.mean(yh * yh, axis=-1, keepdims=True) + EPS))
        o_ref[rows, :] = jnp.concatenate(halves, axis=1) * ng_ref[...]
        return carry

    lax.fori_loop(0, seq // CHUNK, chunk, 0, unroll=8)


def _ssd(proj, conv_wt, conv_b, v256, layer, batch, seq):
    full = _const_spec
    li = np.arange(CHUNK)
    u_t = np.tile((li[:, None] > li[None, :]).astype(np.float32), (1, SSD_HEADS))
    gm = ((np.arange(128) // 64)[:, None] == (np.arange(256) // 128)[None, :]).astype(np.float32)
    return pl.pallas_call(
        _ssd_kernel,
        grid=(batch,),
        in_specs=[
            pl.BlockSpec((seq, SSD_XBC), lambda b: (b, COL_XBC // SSD_XBC)),
            pl.BlockSpec((seq, SSD_INNER), lambda b: (b, COL_Z // SSD_INNER)),
            pl.BlockSpec((seq, D_BRANCH), lambda b: (b, COL_DT // D_BRANCH)),
            _layer_spec((SSD_CONV, SSD_XBC), layer), _layer_spec((1, SSD_XBC), layer),
            _layer_spec((4, D_BRANCH), layer),
            full((CHUNK, CHUNK)), full((CHUNK, D_BRANCH)),
            full((128, D_BRANCH)), full((4, D_BRANCH)),
        ],
        out_specs=pl.BlockSpec((seq, D_BRANCH), lambda b: (b, 0)),
        out_shape=jax.ShapeDtypeStruct((batch * seq, D_BRANCH), F32),
        scratch_shapes=[pltpu.VMEM((2 * SSD_DSTATE, D_BRANCH), F32)],
        compiler_params=_cparams(),
        name="ssd",
    )(proj, proj, proj, conv_wt, conv_b, v256,
      _tri64(), jnp.asarray(u_t), jnp.asarray(gm), _head_masks())


S5_RT = 256


def _gelu_tanh(x):
    return 0.5 * x * (1.0 + jnp.tanh(math.sqrt(2.0 / math.pi) * (x + 0.044715 * (x * x * x))))


def _s5_kernel(ua_ref, ub_ref, kbd_ref, wst_ref, wout_ref, lam_ref, d_ref, wglu_ref, o_ref,
               hloc_s, hprev_s, ya_s, yb_s):
    u_refs = (ua_ref, ub_ref)
    y_refs = (ya_s, yb_s)
    seq = ua_ref.shape[0]
    nchunk = seq // S5_L
    hs = S5_HALF_STATE
    rowmod = lax.broadcasted_iota(jnp.int32, (S5_RT, D_BRANCH), 0) % S5_L

    def tile(i, carry):
        rows = pl.ds(pl.multiple_of(i * S5_RT, S5_RT), S5_RT)
        ut = jnp.concatenate([ua_ref[rows, :], ub_ref[rows, :]], axis=1)
        acc = _dot(ut.astype(BF16), kbd_ref[0])
        for d in range(1, S5_L):
            ud = jnp.where(rowmod >= d, pltpu.roll(ut, d, axis=0), 0.0)
            acc = acc + _dot(ud.astype(BF16), kbd_ref[d])
        ya_s[rows, :] = acc[:, 0:128]
        yb_s[rows, :] = acc[:, 128:256]
        return carry

    lax.fori_loop(0, seq // S5_RT, tile, 0)

    for half in range(2):
        acc = None
        for s in range(0, S5_L, 2):
            us = jnp.concatenate([u_refs[half][pl.ds(s + k, nchunk, stride=S5_L), :] for k in range(2)],
                                 axis=1)
            w2 = wst_ref[half, s:s + 2].reshape(2 * 128, 2 * S5_HALF_STATE)
            term = _dot(us.astype(BF16), w2)
            acc = term if acc is None else acc + term
        hloc_s[half] = acc

    lam = [(lam_ref[half, 0:1, :], lam_ref[half, 1:2, :]) for half in range(2)]

    def step(c, carry):
        out = []
        for half in range(2):
            hr, hi = carry[half]
            lr, li = lam[half]
            hprev_s[half, pl.ds(c, 1), 0:hs] = hr
            hprev_s[half, pl.ds(c, 1), hs:2 * hs] = hi
            loc = hloc_s[half, pl.ds(c, 1), :]
            out.append((lr * hr - li * hi + loc[:, 0:hs], lr * hi + li * hr + loc[:, hs:2 * hs]))
        return tuple(out)

    zero = jnp.zeros((1, hs), F32)
    lax.fori_loop(0, nchunk, step, ((zero, zero), (zero, zero)), unroll=2)

    for half in range(2):
        hp = hprev_s[half].astype(BF16)
        for s in range(0, S5_L, 2):
            w2 = wout_ref[half, s:s + 2].reshape(2 * 128, 2 * S5_HALF_STATE)
            yy = _dot_nt(hp, w2)
            for k in range(2):
                srows = pl.ds(s + k, nchunk, stride=S5_L)
                y_refs[half][srows, :] = y_refs[half][srows, :] + yy[:, k * 128:(k + 1) * 128]

    def tail(i, carry):
        rows = pl.ds(pl.multiple_of(i * S5_RT, S5_RT), S5_RT)
        ut = jnp.concatenate([ua_ref[rows, :], ub_ref[rows, :]], axis=1)
        yt = jnp.concatenate([ya_s[rows, :], yb_s[rows, :]], axis=1)
        y = _gelu_tanh(yt + d_ref[...] * ut)
        o_ref[rows, :] = y * _sigmoid(_dot(y.astype(BF16), wglu_ref[...]))
        return carry

    lax.fori_loop(0, seq // S5_RT, tail, 0)


def _s5_weights(a_re, a_im, b_re, b_im, c_re, c_im, log_dt):
    hp = lax.Precision.HIGHEST
    step = jnp.exp(log_dt)[:, None]
    mag = jnp.exp(a_re * step)
    lam_re = mag * jnp.cos(a_im * step)
    lam_im = mag * jnp.sin(a_im * step)
    den = a_re * a_re + a_im * a_im
    num_re = lam_re - 1.0
    coef_re = (num_re * a_re + lam_im * a_im) / den
    coef_im = (lam_im * a_re - num_re * a_im) / den
    bb_re = coef_re[..., None] * b_re - coef_im[..., None] * b_im
    bb_im = coef_re[..., None] * b_im + coef_im[..., None] * b_re

    pw_re = [jnp.ones_like(lam_re)]
    pw_im = [jnp.zeros_like(lam_im)]
    for _ in range(S5_L):
        pr, pi = pw_re[-1], pw_im[-1]
        pw_re.append(pr * lam_re - pi * lam_im)
        pw_im.append(pr * lam_im + pi * lam_re)
    rev_re = jnp.stack(pw_re[S5_L - 1::-1])
    rev_im = jnp.stack(pw_im[S5_L - 1::-1])
    pw_re = jnp.stack(pw_re)
    pw_im = jnp.stack(pw_im)

    bbt_re = jnp.swapaxes(bb_re, 1, 2)[None]
    bbt_im = jnp.swapaxes(bb_im, 1, 2)[None]
    cp_re = c_re[None] * pw_re[:, :, None, :] - c_im[None] * pw_im[:, :, None, :]
    cp_im = c_re[None] * pw_im[:, :, None, :] + c_im[None] * pw_re[:, :, None, :]
    kd = jnp.einsum('gjq,dgiq->dgji', jnp.concatenate([bbt_re[0], bbt_im[0]], axis=-1),
                    jnp.concatenate([cp_re[:S5_L], -cp_im[:S5_L]], axis=-1), precision=hp)

    def halves(t):
        return jnp.moveaxis(t.reshape((S5_L, 2, S5_HALF_G) + t.shape[2:]), 1, 0)

    rr, ri_ = rev_re[:, :, None, :], rev_im[:, :, None, :]
    wbt = jnp.stack([halves(rr * bbt_re - ri_ * bbt_im),
                     halves(rr * bbt_im + ri_ * bbt_re)], axis=4)
    cpn = jnp.stack([halves(cp_re[1:S5_L + 1]), halves(-cp_im[1:S5_L + 1])], axis=4)
    compact = jnp.stack([wbt, cpn]).reshape(2, 2, S5_L, S5_HALF_G, S5_GROUP_CH, 2 * S5_STATE)
    lam_l = jnp.stack([pw_re[S5_L].reshape(2, S5_HALF_STATE),
                       pw_im[S5_L].reshape(2, S5_HALF_STATE)], axis=1)
    return kd, compact, lam_l


def _s5_expand_kernel(c_ref, o_ref):
    o_ref[...] = jnp.zeros(o_ref.shape, BF16)
    for s in range(S5_L):
        for g in range(S5_HALF_G):
            blk = c_ref[s, g]
            for ri in range(2):
                lo = ri * S5_HALF_STATE + g * S5_STATE
                o_ref[s, g * S5_GROUP_CH:(g + 1) * S5_GROUP_CH, lo:lo + S5_STATE] = (
                    blk[:, ri * S5_STATE:(ri + 1) * S5_STATE].astype(BF16))


def _s5_kbd_kernel(k_ref, o_ref):
    o_ref[...] = jnp.zeros(o_ref.shape, BF16)
    for d in range(S5_L):
        for g in range(S5_GROUPS):
            lo = g * S5_GROUP_CH
            o_ref[d, lo:lo + S5_GROUP_CH, lo:lo + S5_GROUP_CH] = k_ref[d, g].astype(BF16)


def _s5_kbd(kd):
    depth = kd.shape[0]
    return pl.pallas_call(
        _s5_kbd_kernel,
        grid=(depth,),
        in_specs=[pl.BlockSpec((None, S5_L, S5_GROUPS, S5_GROUP_CH, S5_GROUP_CH), lambda l: (l, 0, 0, 0, 0))],
        out_specs=pl.BlockSpec((None, S5_L, D_BRANCH, D_BRANCH), lambda l: (l, 0, 0, 0)),
        out_shape=jax.ShapeDtypeStruct((depth, S5_L, D_BRANCH, D_BRANCH), BF16),
        compiler_params=_cparams(),
        name="s5_kbd",
    )(kd)


def _s5_expand(compact):
    depth = compact.shape[0]
    blk_in = (None, None, None, S5_L, S5_HALF_G, S5_GROUP_CH, 2 * S5_STATE)
    blk_out = (None, None, None, S5_L, S5_HALF_G * S5_GROUP_CH, 2 * S5_HALF_STATE)
    idx = lambda l, w, h: (l, w, h, 0, 0, 0, 0)
    return pl.pallas_call(
        _s5_expand_kernel,
        grid=(depth, 2, 2),
        in_specs=[pl.BlockSpec(blk_in, idx)],
        out_specs=pl.BlockSpec(blk_out, lambda l, w, h: (l, w, h, 0, 0, 0)),
        out_shape=jax.ShapeDtypeStruct((depth, 2, 2, S5_L, 128, 2 * S5_HALF_STATE), BF16),
        compiler_params=_cparams(3),
        name="s5_expand",
    )(compact)


def _s5(proj, kbd, wmats, lam_l, d_skip, w_glu, layer, batch, seq):
    nchunk = seq // S5_L
    wspec = lambda which: pl.BlockSpec(
        (None, None, 2, S5_L, 128, 2 * S5_HALF_STATE), lambda b: (layer, which, 0, 0, 0, 0),
        pipeline_mode=pl.Buffered(1))
    return pl.pallas_call(
        _s5_kernel,
        grid=(batch,),
        in_specs=[
            pl.BlockSpec((seq, 128), lambda b: (b, COL_U // 128)),
            pl.BlockSpec((seq, 128), lambda b: (b, COL_U // 128 + 1)),
            _layer_spec((S5_L, D_BRANCH, D_BRANCH), layer),
            wspec(0), wspec(1),
            _layer_spec((2, 2, S5_HALF_STATE), layer),
            _layer_spec((1, D_BRANCH), layer),
            _layer_spec((D_BRANCH, D_BRANCH), layer),
        ],
        out_specs=pl.BlockSpec((seq, D_BRANCH), lambda b: (b, 0)),
        out_shape=jax.ShapeDtypeStruct((batch * seq, D_BRANCH), F32),
        scratch_shapes=[
            pltpu.VMEM((2, nchunk, 2 * S5_HALF_STATE), F32),
            pltpu.VMEM((2, nchunk, 2 * S5_HALF_STATE), F32),
            pltpu.VMEM((seq, 128), F32),
            pltpu.VMEM((seq, 128), F32),
        ],
        compiler_params=_cparams(),
        name="s5",
    )(proj, proj, kbd, wmats, wmats, lam_l, d_skip, w_glu)


def _attn_kernel(q_ref, k_ref, v_ref, gains_ref, bias_ref, onesbd_ref, hm_ref, o_ref,
                 kp_s, vp_s, qn_s):
    seq = q_ref.shape[0]
    qg_ref = gains_ref.at[0:1, :]
    kg_ref = gains_ref.at[1:2, :]
    kp_s[0:ATT_PAD, :] = jnp.zeros((ATT_PAD, D_BRANCH), BF16)
    vp_s[0:ATT_PAD, :] = jnp.zeros((ATT_PAD, D_BRANCH), BF16)
    rt = 256
    scale = ATT_HEADDIM ** -0.5

    def prep(i, carry):
        rows = pl.ds(pl.multiple_of(i * rt, rt), rt)
        k = k_ref[rows, :]
        ms = _dot_exact_rhs(k * k, onesbd_ref[...]) * (1.0 / ATT_HEADDIM)
        kn = k * lax.rsqrt(ms + EPS) * kg_ref[...]
        prow = pl.ds(pl.multiple_of(ATT_PAD + i * rt, 64), rt)
        kp_s[prow, :] = kn.astype(BF16)
        vp_s[prow, :] = v_ref[rows, :].astype(BF16)
        q = q_ref[rows, :]
        ms = _dot_exact_rhs(q * q, onesbd_ref[...]) * (1.0 / ATT_HEADDIM)
        qn_s[rows, :] = q * lax.rsqrt(ms + EPS) * (qg_ref[...] * scale)
        return carry

    lax.fori_loop(0, seq // rt, prep, 0)

    pidx = lax.broadcasted_iota(jnp.int32, (ATT_HEADS * CHUNK, ATT_BAND), 1)

    def chunk(c, carry):
        r0 = pl.multiple_of(c * CHUNK, CHUNK)
        qn = qn_s[pl.ds(r0, CHUNK), :]
        qs = jnp.concatenate([qn * hm_ref[h:h + 1, :] for h in range(ATT_HEADS)], axis=0).astype(BF16)
        kb = kp_s[pl.ds(r0, ATT_BAND), :]
        vb = vp_s[pl.ds(r0, ATT_BAND), :]
        s = _dot_nt(qs, kb)
        first_valid = jnp.maximum(ATT_LEFT_CHUNKS + 1 - c, 1) * CHUNK
        s = jnp.where(pidx >= first_valid, s + bias_ref[...], NEG_INF)
        m = jnp.max(s, axis=-1, keepdims=True)
        e = jnp.exp(s - m)
        l = jnp.sum(e, axis=-1, keepdims=True)
        pv = _dot(e.astype(BF16), vb) / l
        out = pv[0:CHUNK, :] * hm_ref[0:1, :]
        for h in range(1, ATT_HEADS):
            out = out + pv[h * CHUNK:(h + 1) * CHUNK, :] * hm_ref[h:h + 1, :]
        o_ref[pl.ds(r0, CHUNK), :] = out
        return carry

    lax.fori_loop(0, seq // CHUNK, chunk, 0, unroll=8)


def _attn_bias(rel_bias):
    n = CHUNK + ATT_BAND + 1
    j = np.arange(n)
    delta = np.where(j < ATT_BAND + 2, -j, n - j)
    rel = np.clip(delta + (ATT_LEFT_CHUNKS + 1) * CHUNK, -ATT_MAX_REL, ATT_MAX_REL) + ATT_MAX_REL
    d = rel_bias[:, rel]
    flat = jnp.tile(d, (1, CHUNK))[:, :CHUNK * (n - 1)]
    bias = flat.reshape(ATT_HEADS, CHUNK, n - 1)[:, :, :ATT_BAND]
    return bias.reshape(ATT_HEADS * CHUNK, ATT_BAND)


def _attn(proj, gains, bias, layer, batch, seq):
    full = _const_spec
    colblk = lambda j: pl.BlockSpec((seq, D_BRANCH), lambda b: (b, COL_QKV // D_BRANCH + j))
    return pl.pallas_call(
        _attn_kernel,
        grid=(batch,),
        in_specs=[
            colblk(0), colblk(1), colblk(2),
            _layer_spec((2, D_BRANCH), layer),
            _layer_spec((ATT_HEADS * CHUNK, ATT_BAND), layer),
            full((D_BRANCH, D_BRANCH)), full((4, D_BRANCH)),
        ],
        out_specs=pl.BlockSpec((seq, D_BRANCH), lambda b: (b, 0)),
        out_shape=jax.ShapeDtypeStruct((batch * seq, D_BRANCH), F32),
        scratch_shapes=[
            pltpu.VMEM((ATT_PAD + seq, D_BRANCH), BF16),
            pltpu.VMEM((ATT_PAD + seq, D_BRANCH), BF16),
            pltpu.VMEM((seq, D_BRANCH), F32),
        ],
        compiler_params=_cparams(),
        name="attn",
    )(proj, proj, proj, gains, bias, _ones_bd(64), _head_masks())


def kernel(x, p, norm_mix, w_in, hg_lb_logits, hg_o_norm, ssd_conv_w, ssd_conv_b, ssd_dt_bias, ssd_A_log, ssd_D, ssd_norm, s5_A_re, s5_A_im, s5_B_re, s5_B_im, s5_C_re, s5_C_im, s5_D, s5_log_dt, s5_w_glu, att_q_norm, att_k_norm, att_rel_bias, w_branch, w_out, norm_ffn, w_ff1, w_ff2, w_ple, norm_ple, w_ple_gate):
    batch, seq, _ = x.shape
    depth = w_in.shape[0]
    t = batch * seq
    row3 = lambda a: a.reshape(depth, 1, -1)

    lb_all = jnp.cumsum(jax.nn.softmax(hg_lb_logits.astype(F32), axis=0), axis=0)
    lb_all = lb_all - lb_all[0:1]
    hg_vec = jnp.stack([jnp.log(lb_all), jnp.log1p(-lb_all), hg_o_norm], axis=1)
    spread = lambda v: jnp.repeat(v, SSD_HEADDIM, axis=1)
    ssd_v256 = jnp.stack([spread(ssd_D), ssd_norm, spread(ssd_dt_bias), spread(ssd_A_log)], axis=1)
    conv_wt = jnp.swapaxes(ssd_conv_w, 1, 2)
    s5_kd, s5_compact, lam_l = jax.vmap(_s5_weights)(s5_A_re, s5_A_im, s5_B_re, s5_B_im, s5_C_re,
                                                     s5_C_im, s5_log_dt)
    s5_mats = _s5_expand(s5_compact)
    kbd = _s5_kbd(s5_kd)
    att_gains = jnp.stack([jnp.tile(att_q_norm, (1, ATT_HEADS)), jnp.tile(att_k_norm, (1, ATT_HEADS))],
                          axis=1)
    att_bias = jax.vmap(_attn_bias)(att_rel_bias)
    w_mix, w_gate = _repack(w_in)
    w_branch16, w_out16, w_glu16 = w_branch.astype(BF16), w_out.astype(BF16), s5_w_glu.astype(BF16)
    w_ff1_16, w_ff2_16 = w_ff1.astype(BF16), w_ff2.astype(BF16)
    w_pg16, w_ple16 = w_ple_gate.astype(BF16), w_ple.astype(BF16)
    g_mix, g_ffn, g_ple = row3(norm_mix), row3(norm_ffn), row3(norm_ple)

    x2 = x.reshape(t, D_MODEL)
    p3 = p.reshape(depth, t, D_PLE)
    for i in range(depth):
        proj = _inproj(x2, g_mix, w_mix, i)
        y_a = _hgrn(proj, hg_vec, i, batch, seq)
        y_b = _ssd(proj, conv_wt, row3(ssd_conv_b), ssd_v256, i, batch, seq)
        y_c = _s5(proj, kbd, s5_mats, lam_l, row3(s5_D), w_glu16, i, batch, seq)
        y_d = _attn(proj, att_gains, att_bias, i, batch, seq)
        x2 = _merge(x2, y_a, y_b, y_c, y_d, g_mix, w_gate, w_branch16, w_out16, i)
        x2 = _ffn_ple(x2, p3, g_ffn, w_ff1_16, w_ff2_16, g_ple, w_pg16, w_ple16, i)
    return x2.reshape(batch, seq, D_MODEL)
```

```python
import math

import jax
import jax.numpy as jnp
import numpy as np
from jax import lax
from jax.experimental import pallas as pl
from jax.experimental.pallas import tpu as pltpu

F32 = jnp.float32
BF16 = jnp.bfloat16

D_MODEL = 1024
CHUNK = 64
D_PLE = 256
N_BRANCH = 4
D_BRANCH = 256
D_FF = 4096
EPS = 1e-6
NEG_INF = -1e30
LOG2E = math.log2(math.e)

HG_HEADS = 4
HG_DK = 64
SSD_HEADS = 4
SSD_HEADDIM = 64
SSD_GROUPS = 2
SSD_DSTATE = 64
SSD_CONV = 4
SSD_INNER = 256
SSD_XBC = 512
S5_GROUP_CH = 16
S5_GROUPS = 16
S5_STATE = 64
ATT_HEADS = 4
ATT_HEADDIM = 64
ATT_LEFT_CHUNKS = 8
ATT_MAX_REL = 128

COL_HG = 0
COL_XBC = 1024
COL_Z = 1536
COL_U = 1792
COL_QKV = 2048
COL_DT = 2816
N_MIX = 3072
DT_PAD = 128

W_HG, W_Z, W_XBC, W_DT, W_U, W_QKV, W_GATE = 0, 1024, 1280, 1792, 1796, 2052, 2820
N_IN = 6916

S5_L = 8
S5_HALF_G = 8
S5_HALF_STATE = S5_HALF_G * S5_STATE

ATT_BAND = (ATT_LEFT_CHUNKS + 2) * CHUNK
ATT_PAD = (ATT_LEFT_CHUNKS + 1) * CHUNK

VMEM_LIMIT = 56 * 1024 * 1024


def _cparams(n_axes=1):
    return pltpu.CompilerParams(
        dimension_semantics=("arbitrary",) * n_axes, vmem_limit_bytes=VMEM_LIMIT)


def _layer_spec(shape, layer):
    return pl.BlockSpec((None,) + tuple(shape), lambda *_: (layer,) + (0,) * len(shape),
                        pipeline_mode=pl.Buffered(1))


def _const_spec(shape):
    return pl.BlockSpec(tuple(shape), lambda *_: (0,) * len(shape), pipeline_mode=pl.Buffered(1))


def _dot(a, b):
    return jnp.dot(a, b, preferred_element_type=F32)


def _dot_nt(a, b):
    return lax.dot_general(a, b, (((1,), (1,)), ((), ())), preferred_element_type=F32)


def _dot_tn(a, b):
    return lax.dot_general(a, b, (((0,), (0,)), ((), ())), preferred_element_type=F32)


def _split3(a):
    a1 = a.astype(BF16)
    r1 = a - a1.astype(F32)
    a2 = r1.astype(BF16)
    r2 = r1 - a2.astype(F32)
    return a1, a2, r2.astype(BF16)


def _dot_exact_rhs(a, m):
    a1, a2, a3 = _split3(a)
    return _dot(a1, m) + _dot(a2, m) + _dot(a3, m)


def _dot_exact_lhs(m, a):
    a1, a2, a3 = _split3(a)
    return _dot(m, a1) + _dot(m, a2) + _dot(m, a3)


def _sigmoid(x):
    return 1.0 / (1.0 + jnp.exp(-x))


def _softplus(x):
    return jnp.maximum(x, 0.0) + jnp.log(1.0 + jnp.exp(-jnp.abs(x)))


def _rms_rows(x, g):
    return x * lax.rsqrt(jnp.mean(x * x, axis=-1, keepdims=True) + EPS) * g


REPACK_TN = 256
MIX_TILE_SRC = (W_HG, W_HG + 256, W_HG + 512, W_HG + 768, W_XBC, W_XBC + 256, W_Z, W_U,
                W_QKV, W_QKV + 256, W_QKV + 512, W_DT)
DT_TILE = COL_DT // REPACK_TN


def _repack_mix_kernel(src_ref, w_ref, sel_ref, o_ref):
    del src_ref
    j = pl.program_id(0)
    for l in range(o_ref.shape[0]):
        wt = w_ref[:, l, :].T.astype(BF16)

        @pl.when(j != DT_TILE)
        def _():
            o_ref[l] = wt

        @pl.when(j == DT_TILE)
        def _():
            o_ref[l] = _dot(wt, sel_ref[...]).astype(BF16)


def _repack_gate_kernel(w_ref, o_ref):
    for l in range(o_ref.shape[0]):
        o_ref[l] = w_ref[:, l, :].T.astype(BF16)


def _repack(w_in):
    depth = w_in.shape[0]
    w_t = jnp.transpose(w_in, (2, 0, 1))
    sel = np.zeros((REPACK_TN, D_BRANCH), np.float32)
    for h in range(SSD_HEADS):
        sel[h, h * 64:(h + 1) * 64] = 1.0
    w_mix = pl.pallas_call(
        _repack_mix_kernel,
        grid_spec=pltpu.PrefetchScalarGridSpec(
            num_scalar_prefetch=1,
            grid=(N_MIX // REPACK_TN,),
            in_specs=[pl.BlockSpec((pl.Element(REPACK_TN), pl.Element(depth), pl.Element(D_MODEL)),
                                   lambda j, src: (src[j], 0, 0)),
                      pl.BlockSpec((REPACK_TN, D_BRANCH), lambda j, src: (0, 0))],
            out_specs=pl.BlockSpec((depth, D_MODEL, REPACK_TN), lambda j, src: (0, 0, j)),
        ),
        out_shape=jax.ShapeDtypeStruct((depth, D_MODEL, N_MIX), BF16),
        compiler_params=_cparams(),
        name="repack_mix",
    )(jnp.asarray(MIX_TILE_SRC, jnp.int32), w_t, jnp.asarray(sel, BF16))
    n_gate = N_BRANCH * D_MODEL
    w_gate = pl.pallas_call(
        _repack_gate_kernel,
        grid=(n_gate // REPACK_TN,),
        in_specs=[pl.BlockSpec((pl.Element(REPACK_TN), pl.Element(depth), pl.Element(D_MODEL)),
                               lambda j: (W_GATE + j * REPACK_TN, 0, 0))],
        out_specs=pl.BlockSpec((depth, D_MODEL, REPACK_TN), lambda j: (0, 0, j)),
        out_shape=jax.ShapeDtypeStruct((depth, D_MODEL, n_gate), BF16),
        compiler_params=_cparams(),
        name="repack_gate",
    )(w_t)
    return w_mix, w_gate


def _inproj_kernel(x_ref, g_ref, w_ref, o_ref):
    h = _rms_rows(x_ref[...], g_ref[...])
    o_ref[...] = _dot(h.astype(BF16), w_ref[...])


def _inproj(x2, g, w_mix, layer, tm=1024):
    t = x2.shape[0]
    return pl.pallas_call(
        _inproj_kernel,
        grid=(t // tm,),
        in_specs=[
            pl.BlockSpec((tm, D_MODEL), lambda i: (i, 0)),
            _layer_spec((1, D_MODEL), layer),
            _layer_spec((D_MODEL, N_MIX), layer),
        ],
        out_specs=pl.BlockSpec((tm, N_MIX), lambda i: (i, 0)),
        out_shape=jax.ShapeDtypeStruct((t, N_MIX), F32),
        compiler_params=_cparams(),
        name="inproj",
    )(x2, g, w_mix)


def _merge_kernel(x_ref, ya_ref, yb_ref, yc_ref, yd_ref, g_ref, wg_ref, wb_ref, wo_ref, o_ref):
    x = x_ref[...]
    h = _rms_rows(x, g_ref[...]).astype(BF16)
    merged = None
    for m, y_ref in enumerate((ya_ref, yb_ref, yc_ref, yd_ref)):
        gate = _sigmoid(_dot(h, wg_ref[:, m * D_MODEL:(m + 1) * D_MODEL]))
        term = gate * _dot(y_ref[...].astype(BF16), wb_ref[m])
        merged = term if merged is None else merged + term
    o_ref[...] = x + _dot(merged.astype(BF16), wo_ref[...])


def _merge(x2, ya, yb, yc, yd, g, w_gate, w_branch, w_out, layer, tm=512):
    t = x2.shape[0]
    row = lambda i: (i, 0)
    return pl.pallas_call(
        _merge_kernel,
        grid=(t // tm,),
        in_specs=[
            pl.BlockSpec((tm, D_MODEL), row),
            pl.BlockSpec((tm, D_BRANCH), row),
            pl.BlockSpec((tm, D_BRANCH), row),
            pl.BlockSpec((tm, D_BRANCH), row),
            pl.BlockSpec((tm, D_BRANCH), row),
            _layer_spec((1, D_MODEL), layer),
            _layer_spec((D_MODEL, N_BRANCH * D_MODEL), layer),
            _layer_spec((N_BRANCH, D_BRANCH, D_MODEL), layer),
            _layer_spec((D_MODEL, D_MODEL), layer),
        ],
        out_specs=pl.BlockSpec((tm, D_MODEL), row),
        out_shape=jax.ShapeDtypeStruct((t, D_MODEL), F32),
        compiler_params=_cparams(),
        name="merge",
    )(x2, ya, yb, yc, yd, g, w_gate, w_branch, w_out)


def _ffn_ple_kernel(x_ref, p_ref, g1_ref, w1_ref, w2_ref, g2_ref, wg_ref, wp_ref, o_ref):
    x = x_ref[...]
    h = _rms_rows(x, g1_ref[...]).astype(BF16)
    acc = x
    tk = 1024
    for j in range(D_FF // tk):
        a = jnp.maximum(_dot(h, w1_ref[:, j * tk:(j + 1) * tk]), 0.0)
        acc = acc + _dot((a * a).astype(BF16), w2_ref[j * tk:(j + 1) * tk, :])
    h2 = _rms_rows(acc, g2_ref[...]).astype(BF16)
    gate = _sigmoid(_dot(h2, wg_ref[...]))
    o_ref[...] = acc + _dot(p_ref[...].astype(BF16), wp_ref[...]) * gate


def _ffn_ple(x2, p3, g_ffn, w1, w2, g_ple, w_gate, w_ple, layer, tm=1024):
    t = x2.shape[0]
    return pl.pallas_call(
        _ffn_ple_kernel,
        grid=(t // tm,),
        in_specs=[
            pl.BlockSpec((tm, D_MODEL), lambda i: (i, 0)),
            pl.BlockSpec((None, tm, D_PLE), lambda i: (layer, i, 0)),
            _layer_spec((1, D_MODEL), layer),
            _layer_spec((D_MODEL, D_FF), layer),
            _layer_spec((D_FF, D_MODEL), layer),
            _layer_spec((1, D_MODEL), layer),
            _layer_spec((D_MODEL, D_MODEL), layer),
            _layer_spec((D_PLE, D_MODEL), layer),
        ],
        out_specs=pl.BlockSpec((tm, D_MODEL), lambda i: (i, 0)),
        out_shape=jax.ShapeDtypeStruct((t, D_MODEL), F32),
        compiler_params=_cparams(),
        name="ffn_ple",
    )(x2, p3, g_ffn, w1, w2, g_ple, w_gate, w_ple)


def _tri64():
    i = np.arange(CHUNK)
    return jnp.asarray(i[:, None] >= i[None, :], BF16)


def _ones_bd(block, n=D_BRANCH):
    i = np.arange(n) // block
    return jnp.asarray(i[:, None] == i[None, :], BF16)


def _head_masks():
    lane = np.arange(D_BRANCH) // 64
    return jnp.asarray(lane[None, :] == np.arange(4)[:, None], F32)


HG_SUB = 16
HG_TILE = 256
HG_SUB_STACK = 8 * HG_SUB + 4 * HG_SUB


def _hgrn_kernel(p_ref, vec_ref, tri_ref, onesbd_ref, bdmask_ref, hm_ref, o_ref, st_ref, bl_s, ck_s, w_s):
    seq = p_ref.shape[0]
    half = HG_SUB // 2
    loglb = vec_ref[0:1, :]
    log1mlb = vec_ref[1:2, :]
    gain = vec_ref[2:3, :]

    def pre(i, carry):
        rows = pl.ds(pl.multiple_of(i * HG_TILE, HG_TILE), HG_TILE)
        z = p_ref[rows, 256:512]
        ls = jnp.minimum(z, 0.0) - jnp.log(1.0 + jnp.exp(-jnp.abs(z)))
        y = log1mlb + ls
        lf = jnp.maximum(loglb, y) + jnp.log(1.0 + jnp.exp(-jnp.abs(loglb - y)))
        bl2 = _dot_exact_lhs(tri_ref[...], lf) * LOG2E
        bl_s[rows, :] = bl2
        ck_s[rows, :] = bl2 - (log1mlb + (ls - z)) * LOG2E
        return carry

    lax.fori_loop(0, seq // HG_TILE, pre, 0, unroll=2)

    row16 = lax.broadcasted_iota(jnp.int32, (HG_SUB, D_BRANCH), 0)
    row8 = lax.broadcasted_iota(jnp.int32, (half, D_BRANCH), 0)

    def chunk(c, carry):
        r0 = c * CHUNK
        subs = []
        for s in range(CHUNK // HG_SUB):
            rs = pl.multiple_of(r0 + s * HG_SUB, HG_SUB)
            rows = pl.ds(rs, HG_SUB)
            q = p_ref[rows, 0:256]
            bl = bl_s[rows, :]
            base = s * HG_SUB_STACK
            for j in range(half):
                d = jnp.where(row16 >= j, bl - ck_s[pl.ds(rs + j, 1), :], NEG_INF)
                w_s[base + j * HG_SUB:base + (j + 1) * HG_SUB, :] = (q * jnp.exp2(d)).astype(BF16)
            q_hi = q[half:, :]
            bl_hi = bl[half:, :]
            for jp in range(half // 2):
                pieces = []
                for j in (half + 2 * jp, half + 2 * jp + 1):
                    d = jnp.where(row8 >= j - half, bl_hi - ck_s[pl.ds(rs + j, 1), :], NEG_INF)
                    pieces.append(q_hi * jnp.exp2(d))
                off = base + half * HG_SUB + jp * HG_SUB
                w_s[off:off + HG_SUB, :] = jnp.concatenate(pieces, axis=0).astype(BF16)
            subs.append((rs, rows, q, bl))

        r = _dot(w_s[...], onesbd_ref[...])

        for s, (rs, rows, q, bl) in enumerate(subs):
            base = s * HG_SUB_STACK
            o_lo = jnp.zeros((half, D_BRANCH), F32)
            o_hi = jnp.zeros((half, D_BRANCH), F32)
            for j in range(HG_SUB):
                vj = p_ref[pl.ds(rs + j, 1), 512:768]
                if j < half:
                    o_lo = o_lo + r[base + j * HG_SUB:base + j * HG_SUB + half, :] * vj
                    o_hi = o_hi + r[base + j * HG_SUB + half:base + (j + 1) * HG_SUB, :] * vj
                else:
                    off = base + half * HG_SUB + (j - half) * half
                    o_hi = o_hi + r[off:off + half, :] * vj
            o_ref[rows, :] = jnp.concatenate([o_lo, o_hi], axis=0)
        return carry

    lax.fori_loop(0, seq // CHUNK, chunk, 0, unroll=2)

    st_ref[...] = jnp.zeros_like(st_ref)
    nsub = CHUNK // HG_SUB

    def carry_state(c, carry):
        rows = pl.ds(pl.multiple_of(c * CHUNK, CHUNK), CHUNK)
        q = p_ref[rows, 0:256]
        v = p_ref[rows, 512:768].astype(BF16)
        b = bl_s[rows, :]
        ck = ck_s[rows, :]
        ends = [b[(s + 1) * HG_SUB - 1:(s + 1) * HG_SUB, :] for s in range(nsub)]
        st = st_ref[...]
        o = _dot_nt((q * jnp.exp2(b)).astype(BF16), st.astype(BF16))
        k_end = jnp.exp2(ends[-1] - ck).astype(BF16)
        st_ref[...] = st * jnp.exp2(ends[-1]) + _dot_tn(v, k_end) * bdmask_ref[...]
        for s in range(nsub - 1):
            sub = slice(s * HG_SUB, (s + 1) * HG_SUB)
            after = slice((s + 1) * HG_SUB, CHUNK)
            n = CHUNK - (s + 1) * HG_SUB
            qd = q[after, :] * jnp.exp2(b[after, :] - ends[s])
            qs = jnp.concatenate([qd * hm_ref[h:h + 1, :] for h in range(HG_HEADS)], axis=0)
            ks = jnp.exp2(ends[s] - ck[sub, :]).astype(BF16)
            sc = _dot_nt(qs.astype(BF16), ks)
            pv = _dot(sc.astype(BF16), v[sub, :])
            add = pv[0:n, :] * hm_ref[0:1, :]
            for h in range(1, HG_HEADS):
                add = add + pv[h * n:(h + 1) * n, :] * hm_ref[h:h + 1, :]
            o = jnp.concatenate([o[:(s + 1) * HG_SUB, :], o[after, :] + add], axis=0)
        o_ref[rows, :] = o_ref[rows, :] + o
        return carry

    lax.fori_loop(0, seq // CHUNK, carry_state, 0, unroll=8)

    def post(i, carry):
        rows = pl.ds(pl.multiple_of(i * HG_TILE, HG_TILE), HG_TILE)
        o = o_ref[rows, :]
        g = p_ref[rows, 768:1024]
        ms = _dot_exact_rhs(o * o, onesbd_ref[...]) * (1.0 / HG_DK)
        o_ref[rows, :] = o * lax.rsqrt(ms + EPS) * gain * (g * _sigmoid(g))
        return carry

    lax.fori_loop(0, seq // HG_TILE, post, 0)


def _hgrn(proj, hg_vec, layer, batch, seq):
    bd = _ones_bd(64)
    i = np.arange(HG_TILE)
    tri = jnp.asarray((i[:, None] // CHUNK == i[None, :] // CHUNK) & (i[:, None] >= i[None, :]), BF16)
    return pl.pallas_call(
        _hgrn_kernel,
        grid=(batch,),
        in_specs=[
            pl.BlockSpec((seq, 1024), lambda b: (b, COL_HG // 1024)),
            _layer_spec((3, D_BRANCH), layer),
            _const_spec((HG_TILE, HG_TILE)), _const_spec((D_BRANCH, D_BRANCH)),
            _const_spec((D_BRANCH, D_BRANCH)), _const_spec((4, D_BRANCH)),
        ],
        out_specs=pl.BlockSpec((seq, D_BRANCH), lambda b: (b, 0)),
        out_shape=jax.ShapeDtypeStruct((batch * seq, D_BRANCH), F32),
        scratch_shapes=[
            pltpu.VMEM((D_BRANCH, D_BRANCH), F32),
            pltpu.VMEM((seq, D_BRANCH), F32),
            pltpu.VMEM((seq, D_BRANCH), F32),
            pltpu.VMEM((CHUNK // HG_SUB * HG_SUB_STACK, D_BRANCH), BF16),
        ],
        compiler_params=_cparams(),
        name="hgrn",
    )(proj, hg_vec, tri, bd, bd.astype(F32), _head_masks())


def _ssd_kernel(xbc_ref, z_ref, dt_ref, cw_ref, cb_ref, v256_ref,
                tri_ref, u_ref, gm_ref, hm_ref, o_ref, sn_ref):
    seq = xbc_ref.shape[0]
    dx_ref = v256_ref.at[0:1, :]
    ng_ref = v256_ref.at[1:2, :]
    dtb_ref = v256_ref.at[2:3, :]
    alog_ref = v256_ref.at[3:4, :]
    sn_ref[...] = jnp.zeros_like(sn_ref)
    lidx = lax.broadcasted_iota(jnp.int32, (CHUNK, D_BRANCH), 0)
    sidx = lax.broadcasted_iota(jnp.int32, (CHUNK, D_BRANCH), 1) % CHUNK
    causal = lidx >= sidx
    lane128 = lax.broadcasted_iota(jnp.int32, (1, 2 * SSD_DSTATE), 1)
    a_neg = -jnp.exp(alog_ref[...])

    def chunk(c, carry):
        r0 = pl.multiple_of(c * CHUNK, CHUNK)
        rows = pl.ds(r0, CHUNK)
        cur = xbc_ref[rows, :]
        prev = xbc_ref[pl.ds(pl.multiple_of(jnp.maximum(r0 - 8, 0), 8), 8), :]
        prev = prev * jnp.where(c > 0, 1.0, 0.0)
        ext = jnp.concatenate([prev, cur], axis=0)
        conv = cb_ref[...] + cur * cw_ref[SSD_CONV - 1:SSD_CONV, :]
        for s in range(1, SSD_CONV):
            sh = pltpu.roll(ext, s, axis=0)[8:8 + CHUNK, :]
            conv = conv + sh * cw_ref[SSD_CONV - 1 - s:SSD_CONV - s, :]
        xa = conv * _sigmoid(conv)
        xs = xa[:, 0:SSD_INNER]
        bm = xa[:, SSD_INNER:SSD_INNER + 128]
        cm = xa[:, SSD_INNER + 128:SSD_INNER + 256]

        dtx = _softplus(dt_ref[rows, :] + dtb_ref[...])
        adtx = dtx * a_neg
        cum = _dot_exact_lhs(tri_ref[...], jnp.concatenate([adtx, adtx * u_ref[...]], axis=1))
        acum = cum[:, 0:D_BRANCH]
        seg = cum[:, D_BRANCH:2 * D_BRANCH]
        lmat = jnp.exp(jnp.where(causal, seg, NEG_INF))
        xdt = xs * dtx

        bm16 = bm.astype(BF16)
        bm2 = jnp.concatenate([bm16, bm16], axis=0)
        gcat = jnp.concatenate(
            [_dot_nt(jnp.where((lane128 // SSD_DSTATE) == grp, cm, 0.0).astype(BF16), bm2)
             for grp in range(SSD_GROUPS)], axis=1)
        xbd = jnp.concatenate([xdt * hm_ref[h:h + 1, :] for h in range(SSD_HEADS)], axis=0)
        y = _dot((gcat * lmat).astype(BF16), xbd.astype(BF16))

        sn_old = sn_ref[...]
        y = y + _dot(cm.astype(BF16), sn_old.astype(BF16)) * jnp.exp(acum)
        aend = acum[CHUNK - 1:CHUNK, :]
        upd = _dot_tn(bm16, (jnp.exp(aend - acum) * xdt).astype(BF16))
        sn_ref[...] = jnp.exp(aend) * sn_old + upd * gm_ref[...]

        y = y + dx_ref[...] * xs
        zz = z_ref[rows, :]
        y = y * (zz * _sigmoid(zz))
        halves = []
        for grp in range(SSD_GROUPS):
            yh = y[:, grp * 128:(grp + 1) * 128]
            halves.append(yh * lax.rsqrt(jnp.mean(yh * yh, axis=-1, keepdims=True) + EPS))
        o_ref[rows, :] = jnp.concatenate(halves, axis=1) * ng_ref[...]
        return carry

    lax.fori_loop(0, seq // CHUNK, chunk, 0, unroll=8)


def _ssd(proj, conv_wt, conv_b, v256, layer, batch, seq):
    full = _const_spec
    li = np.arange(CHUNK)
    u_t = np.tile((li[:, None] > li[None, :]).astype(np.float32), (1, SSD_HEADS))
    gm = ((np.arange(128) // 64)[:, None] == (np.arange(256) // 128)[None, :]).astype(np.float32)
    return pl.pallas_call(
        _ssd_kernel,
        grid=(batch,),
        in_specs=[
            pl.BlockSpec((seq, SSD_XBC), lambda b: (b, COL_XBC // SSD_XBC)),
            pl.BlockSpec((seq, SSD_INNER), lambda b: (b, COL_Z // SSD_INNER)),
            pl.BlockSpec((seq, D_BRANCH), lambda b: (b, COL_DT // D_BRANCH)),
            _layer_spec((SSD_CONV, SSD_XBC), layer), _layer_spec((1, SSD_XBC), layer),
            _layer_spec((4, D_BRANCH), layer),
            full((CHUNK, CHUNK)), full((CHUNK, D_BRANCH)),
            full((128, D_BRANCH)), full((4, D_BRANCH)),
        ],
        out_specs=pl.BlockSpec((seq, D_BRANCH), lambda b: (b, 0)),
        out_shape=jax.ShapeDtypeStruct((batch * seq, D_BRANCH), F32),
        scratch_shapes=[pltpu.VMEM((2 * SSD_DSTATE, D_BRANCH), F32)],
        compiler_params=_cparams(),
        name="ssd",
    )(proj, proj, proj, conv_wt, conv_b, v256,
      _tri64(), jnp.asarray(u_t), jnp.asarray(gm), _head_masks())


S5_RT = 512


def _gelu_tanh(x):
    return 0.5 * x * (1.0 + jnp.tanh(math.sqrt(2.0 / math.pi) * (x + 0.044715 * (x * x * x))))


def _s5_kernel(ua_ref, ub_ref, kbd_ref, wst_ref, wout_ref, lam_ref, d_ref, wglu_ref, o_ref,
               hloc_s, hprev_s, ya_s, yb_s):
    u_refs = (ua_ref, ub_ref)
    y_refs = (ya_s, yb_s)
    seq = ua_ref.shape[0]
    nchunk = seq // S5_L
    hs = S5_HALF_STATE
    rowmod = lax.broadcasted_iota(jnp.int32, (S5_RT, D_BRANCH), 0) % S5_L

    def tile(i, carry):
        rows = pl.ds(pl.multiple_of(i * S5_RT, S5_RT), S5_RT)
        ut = jnp.concatenate([ua_ref[rows, :], ub_ref[rows, :]], axis=1)
        acc = _dot(ut.astype(BF16), kbd_ref[0])
        for d in range(1, S5_L):
            ud = jnp.where(rowmod >= d, pltpu.roll(ut, d, axis=0), 0.0)
            acc = acc + _dot(ud.astype(BF16), kbd_ref[d])
        ya_s[rows, :] = acc[:, 0:128]
        yb_s[rows, :] = acc[:, 128:256]
        return carry

    lax.fori_loop(0, seq // S5_RT, tile, 0)

    for half in range(2):
        acc = None
        for s in range(0, S5_L, 2):
            us = jnp.concatenate([u_refs[half][pl.ds(s + k, nchunk, stride=S5_L), :] for k in range(2)],
                                 axis=1)
            w2 = wst_ref[half, s:s + 2].reshape(2 * 128, 2 * S5_HALF_STATE)
            term = _dot(us.astype(BF16), w2)
            acc = term if acc is None else acc + term
        hloc_s[half] = acc

    lam = [(lam_ref[half, 0:1, :], lam_ref[half, 1:2, :]) for half in range(2)]

    def step(c, carry):
        out = []
        for half in range(2):
            hr, hi = carry[half]
            lr, li = lam[half]
            hprev_s[half, pl.ds(c, 1), 0:hs] = hr
            hprev_s[half, pl.ds(c, 1), hs:2 * hs] = hi
            loc = hloc_s[half, pl.ds(c, 1), :]
            out.append((lr * hr - li * hi + loc[:, 0:hs], lr * hi + li * hr + loc[:, hs:2 * hs]))
        return tuple(out)

    zero = jnp.zeros((1, hs), F32)
    lax.fori_loop(0, nchunk, step, ((zero, zero), (zero, zero)), unroll=2)

    for half in range(2):
        hp = hprev_s[half].astype(BF16)
        for s in range(0, S5_L, 2):
            w2 = wout_ref[half, s:s + 2].reshape(2 * 128, 2 * S5_HALF_STATE)
            yy = _dot_nt(hp, w2)
            for k in range(2):
                srows = pl.ds(s + k, nchunk, stride=S5_L)
                y_refs[half][srows, :] = y_refs[half][srows, :] + yy[:, k * 128:(k + 1) * 128]

    def tail(i, carry):
        rows = pl.ds(pl.multiple_of(i * S5_RT, S5_RT), S5_RT)
        ut = jnp.concatenate([ua_ref[rows, :], ub_ref[rows, :]], axis=1)
        yt = jnp.concatenate([ya_s[rows, :], yb_s[rows, :]], axis=1)
        y = _gelu_tanh(yt + d_ref[...] * ut)
        o_ref[rows, :] = y * _sigmoid(_dot(y.astype(BF16), wglu_ref[...]))
        return carry

    lax.fori_loop(0, seq // S5_RT, tail, 0)


def _s5_weights(a_re, a_im, b_re, b_im, c_re, c_im, log_dt):
    hp = lax.Precision.HIGHEST
    step = jnp.exp(log_dt)[:, None]
    mag = jnp.exp(a_re * step)
    lam_re = mag * jnp.cos(a_im * step)
    lam_im = mag * jnp.sin(a_im * step)
    den = a_re * a_re + a_im * a_im
    num_re = lam_re - 1.0
    coef_re = (num_re * a_re + lam_im * a_im) / den
    coef_im = (lam_im * a_re - num_re * a_im) / den
    bb_re = coef_re[..., None] * b_re - coef_im[..., None] * b_im
    bb_im = coef_re[..., None] * b_im + coef_im[..., None] * b_re

    pw_re = [jnp.ones_like(lam_re)]
    pw_im = [jnp.zeros_like(lam_im)]
    for _ in range(S5_L):
        pr, pi = pw_re[-1], pw_im[-1]
        pw_re.append(pr * lam_re - pi * lam_im)
        pw_im.append(pr * lam_im + pi * lam_re)
    rev_re = jnp.stack(pw_re[S5_L - 1::-1])
    rev_im = jnp.stack(pw_im[S5_L - 1::-1])
    pw_re = jnp.stack(pw_re)
    pw_im = jnp.stack(pw_im)

    bbt_re = jnp.swapaxes(bb_re, 1, 2)[None]
    bbt_im = jnp.swapaxes(bb_im, 1, 2)[None]
    cp_re = c_re[None] * pw_re[:, :, None, :] - c_im[None] * pw_im[:, :, None, :]
    cp_im = c_re[None] * pw_im[:, :, None, :] + c_im[None] * pw_re[:, :, None, :]
    kd = jnp.einsum('gjq,dgiq->dgji', jnp.concatenate([bbt_re[0], bbt_im[0]], axis=-1),
                    jnp.concatenate([cp_re[:S5_L], -cp_im[:S5_L]], axis=-1), precision=hp)

    def halves(t):
        return jnp.moveaxis(t.reshape((S5_L, 2, S5_HALF_G) + t.shape[2:]), 1, 0)

    rr, ri_ = rev_re[:, :, None, :], rev_im[:, :, None, :]
    wbt = jnp.stack([halves(rr * bbt_re - ri_ * bbt_im),
                     halves(rr * bbt_im + ri_ * bbt_re)], axis=4)
    cpn = jnp.stack([halves(cp_re[1:S5_L + 1]), halves(-cp_im[1:S5_L + 1])], axis=4)
    compact = jnp.stack([wbt, cpn]).reshape(2, 2, S5_L, S5_HALF_G, S5_GROUP_CH, 2 * S5_STATE)
    lam_l = jnp.stack([pw_re[S5_L].reshape(2, S5_HALF_STATE),
                       pw_im[S5_L].reshape(2, S5_HALF_STATE)], axis=1)
    return kd, compact, lam_l


def _s5_expand_kernel(c_ref, o_ref):
    o_ref[...] = jnp.zeros(o_ref.shape, BF16)
    for s in range(S5_L):
        for g in range(S5_HALF_G):
            blk = c_ref[s, g]
            for ri in range(2):
                lo = ri * S5_HALF_STATE + g * S5_STATE
                o_ref[s, g * S5_GROUP_CH:(g + 1) * S5_GROUP_CH, lo:lo + S5_STATE] = (
                    blk[:, ri * S5_STATE:(ri + 1) * S5_STATE].astype(BF16))


def _s5_kbd_kernel(k_ref, o_ref):
    o_ref[...] = jnp.zeros(o_ref.shape, BF16)
    for d in range(S5_L):
        for g in range(S5_GROUPS):
            lo = g * S5_GROUP_CH
            o_ref[d, lo:lo + S5_GROUP_CH, lo:lo + S5_GROUP_CH] = k_ref[d, g].astype(BF16)


def _s5_kbd(kd):
    depth = kd.shape[0]
    return pl.pallas_call(
        _s5_kbd_kernel,
        grid=(depth,),
        in_specs=[pl.BlockSpec((None, S5_L, S5_GROUPS, S5_GROUP_CH, S5_GROUP_CH), lambda l: (l, 0, 0, 0, 0))],
        out_specs=pl.BlockSpec((None, S5_L, D_BRANCH, D_BRANCH), lambda l: (l, 0, 0, 0)),
        out_shape=jax.ShapeDtypeStruct((depth, S5_L, D_BRANCH, D_BRANCH), BF16),
        compiler_params=_cparams(),
        name="s5_kbd",
    )(kd)


def _s5_expand(compact):
    depth = compact.shape[0]
    blk_in = (None, None, None, S5_L, S5_HALF_G, S5_GROUP_CH, 2 * S5_STATE)
    blk_out = (None, None, None, S5_L, S5_HALF_G * S5_GROUP_CH, 2 * S5_HALF_STATE)
    idx = lambda l, w, h: (l, w, h, 0, 0, 0, 0)
    return pl.pallas_call(
        _s5_expand_kernel,
        grid=(depth, 2, 2),
        in_specs=[pl.BlockSpec(blk_in, idx)],
        out_specs=pl.BlockSpec(blk_out, lambda l, w, h: (l, w, h, 0, 0, 0)),
        out_shape=jax.ShapeDtypeStruct((depth, 2, 2, S5_L, 128, 2 * S5_HALF_STATE), BF16),
        compiler_params=_cparams(3),
        name="s5_expand",
    )(compact)


def _s5(proj, kbd, wmats, lam_l, d_skip, w_glu, layer, batch, seq):
    nchunk = seq // S5_L
    wspec = lambda which: pl.BlockSpec(
        (None, None, 2, S5_L, 128, 2 * S5_HALF_STATE), lambda b: (layer, which, 0, 0, 0, 0),
        pipeline_mode=pl.Buffered(1))
    return pl.pallas_call(
        _s5_kernel,
        grid=(batch,),
        in_specs=[
            pl.BlockSpec((seq, 128), lambda b: (b, COL_U // 128)),
            pl.BlockSpec((seq, 128), lambda b: (b, COL_U // 128 + 1)),
            _layer_spec((S5_L, D_BRANCH, D_BRANCH), layer),
            wspec(0), wspec(1),
            _layer_spec((2, 2, S5_HALF_STATE), layer),
            _layer_spec((1, D_BRANCH), layer),
            _layer_spec((D_BRANCH, D_BRANCH), layer),
        ],
        out_specs=pl.BlockSpec((seq, D_BRANCH), lambda b: (b, 0)),
        out_shape=jax.ShapeDtypeStruct((batch * seq, D_BRANCH), F32),
        scratch_shapes=[
            pltpu.VMEM((2, nchunk, 2 * S5_HALF_STATE), F32),
            pltpu.VMEM((2, nchunk, 2 * S5_HALF_STATE), F32),
            pltpu.VMEM((seq, 128), F32),
            pltpu.VMEM((seq, 128), F32),
        ],
        compiler_params=_cparams(),
        name="s5",
    )(proj, proj, kbd, wmats, wmats, lam_l, d_skip, w_glu)


def _attn_kernel(q_ref, k_ref, v_ref, gains_ref, bias_ref, onesbd_ref, hm_ref, o_ref,
                 kp_s, vp_s, qn_s):
    seq = q_ref.shape[0]
    qg_ref = gains_ref.at[0:1, :]
    kg_ref = gains_ref.at[1:2, :]
    kp_s[0:ATT_PAD, :] = jnp.zeros((ATT_PAD, D_BRANCH), BF16)
    vp_s[0:ATT_PAD, :] = jnp.zeros((ATT_PAD, D_BRANCH), BF16)
    rt = 256
    scale = ATT_HEADDIM ** -0.5

    def prep(i, carry):
        rows = pl.ds(pl.multiple_of(i * rt, rt), rt)
        k = k_ref[rows, :]
        ms = _dot_exact_rhs(k * k, onesbd_ref[...]) * (1.0 / ATT_HEADDIM)
        kn = k * lax.rsqrt(ms + EPS) * kg_ref[...]
        prow = pl.ds(pl.multiple_of(ATT_PAD + i * rt, 64), rt)
        kp_s[prow, :] = kn.astype(BF16)
        vp_s[prow, :] = v_ref[rows, :].astype(BF16)
        q = q_ref[rows, :]
        ms = _dot_exact_rhs(q * q, onesbd_ref[...]) * (1.0 / ATT_HEADDIM)
        qn_s[rows, :] = q * lax.rsqrt(ms + EPS) * (qg_ref[...] * scale)
        return carry

    lax.fori_loop(0, seq // rt, prep, 0, unroll=2)

    pidx = lax.broadcasted_iota(jnp.int32, (ATT_HEADS * CHUNK, ATT_BAND), 1)

    def chunk(c, carry):
        r0 = pl.multiple_of(c * CHUNK, CHUNK)
        qn = qn_s[pl.ds(r0, CHUNK), :]
        qs = jnp.concatenate([qn * hm_ref[h:h + 1, :] for h in range(ATT_HEADS)], axis=0).astype(BF16)
        kb = kp_s[pl.ds(r0, ATT_BAND), :]
        vb = vp_s[pl.ds(r0, ATT_BAND), :]
        s = _dot_nt(qs, kb)
        first_valid = jnp.maximum(ATT_LEFT_CHUNKS + 1 - c, 1) * CHUNK
        s = jnp.where(pidx >= first_valid, s + bias_ref[...], NEG_INF)
        m = jnp.max(s, axis=-1, keepdims=True)
        e = jnp.exp(s - m)
        l = jnp.sum(e, axis=-1, keepdims=True)
        pv = _dot(e.astype(BF16), vb) / l
        out = pv[0:CHUNK, :] * hm_ref[0:1, :]
        for h in range(1, ATT_HEADS):
            out = out + pv[h * CHUNK:(h + 1) * CHUNK, :] * hm_ref[h:h + 1, :]
        o_ref[pl.ds(r0, CHUNK), :] = out
        return carry

    lax.fori_loop(0, seq // CHUNK, chunk, 0, unroll=8)


def _attn_bias(rel_bias):
    n = CHUNK + ATT_BAND + 1
    j = np.arange(n)
    delta = np.where(j < ATT_BAND + 2, -j, n - j)
    rel = np.clip(delta + (ATT_LEFT_CHUNKS + 1) * CHUNK, -ATT_MAX_REL, ATT_MAX_REL) + ATT_MAX_REL
    d = rel_bias[:, rel]
    flat = jnp.tile(d, (1, CHUNK))[:, :CHUNK * (n - 1)]
    bias = flat.reshape(ATT_HEADS, CHUNK, n - 1)[:, :, :ATT_BAND]
    return bias.reshape(ATT_HEADS * CHUNK, ATT_BAND)


def _attn(proj, gains, bias, layer, batch, seq):
    full = _const_spec
    colblk = lambda j: pl.BlockSpec((seq, D_BRANCH), lambda b: (b, COL_QKV // D_BRANCH + j))
    return pl.pallas_call(
        _attn_kernel,
        grid=(batch,),
        in_specs=[
            colblk(0), colblk(1), colblk(2),
            _layer_spec((2, D_BRANCH), layer),
            _layer_spec((ATT_HEADS * CHUNK, ATT_BAND), layer),
            full((D_BRANCH, D_BRANCH)), full((4, D_BRANCH)),
        ],
        out_specs=pl.BlockSpec((seq, D_BRANCH), lambda b: (b, 0)),
        out_shape=jax.ShapeDtypeStruct((batch * seq, D_BRANCH), F32),
        scratch_shapes=[
            pltpu.VMEM((ATT_PAD + seq, D_BRANCH), BF16),
            pltpu.VMEM((ATT_PAD + seq, D_BRANCH), BF16),
            pltpu.VMEM((seq, D_BRANCH), F32),
        ],
        compiler_params=_cparams(),
        name="attn",
    )(proj, proj, proj, gains, bias, _ones_bd(64), _head_masks())


def kernel(x, p, norm_mix, w_in, hg_lb_logits, hg_o_norm, ssd_conv_w, ssd_conv_b, ssd_dt_bias, ssd_A_log, ssd_D, ssd_norm, s5_A_re, s5_A_im, s5_B_re, s5_B_im, s5_C_re, s5_C_im, s5_D, s5_log_dt, s5_w_glu, att_q_norm, att_k_norm, att_rel_bias, w_branch, w_out, norm_ffn, w_ff1, w_ff2, w_ple, norm_ple, w_ple_gate):
    batch, seq, _ = x.shape
    depth = w_in.shape[0]
    t = batch * seq
    row3 = lambda a: a.reshape(depth, 1, -1)

    lb_all = jnp.cumsum(jax.nn.softmax(hg_lb_logits.astype(F32), axis=0), axis=0)
    lb_all = lb_all - lb_all[0:1]
    hg_vec = jnp.stack([jnp.log(lb_all), jnp.log1p(-lb_all), hg_o_norm], axis=1)
    spread = lambda v: jnp.repeat(v, SSD_HEADDIM, axis=1)
    ssd_v256 = jnp.stack([spread(ssd_D), ssd_norm, spread(ssd_dt_bias), spread(ssd_A_log)], axis=1)
    conv_wt = jnp.swapaxes(ssd_conv_w, 1, 2)
    s5_kd, s5_compact, lam_l = jax.vmap(_s5_weights)(s5_A_re, s5_A_im, s5_B_re, s5_B_im, s5_C_re,
                                                     s5_C_im, s5_log_dt)
    s5_mats = _s5_expand(s5_compact)
    kbd = _s5_kbd(s5_kd)
    att_gains = jnp.stack([jnp.tile(att_q_norm, (1, ATT_HEADS)), jnp.tile(att_k_norm, (1, ATT_HEADS))],
                          axis=1)
    att_bias = jax.vmap(_attn_bias)(att_rel_bias)
    w_mix, w_gate = _repack(w_in)
    w_branch16, w_out16, w_glu16 = w_branch.astype(BF16), w_out.astype(BF16), s5_w_glu.astype(BF16)
    w_ff1_16, w_ff2_16 = w_ff1.astype(BF16), w_ff2.astype(BF16)
    w_pg16, w_ple16 = w_ple_gate.astype(BF16), w_ple.astype(BF16)
    g_mix, g_ffn, g_ple = row3(norm_mix), row3(norm_ffn), row3(norm_ple)

    x2 = x.reshape(t, D_MODEL)
    p3 = p.reshape(depth, t, D_PLE)
    for i in range(depth):
        proj = _inproj(x2, g_mix, w_mix, i)
        y_a = _hgrn(proj, hg_vec, i, batch, seq)
        y_b = _ssd(proj, conv_wt, row3(ssd_conv_b), ssd_v256, i, batch, seq)
        y_c = _s5(proj, kbd, s5_mats, lam_l, row3(s5_D), w_glu16, i, batch, seq)
        y_d = _attn(proj, att_gains, att_bias, i, batch, seq)
        x2 = _merge(x2, y_a, y_b, y_c, y_d, g_mix, w_gate, w_branch16, w_out16, i)
        x2 = _ffn_ple(x2, p3, g_ffn, w_ff1_16, w_ff2_16, g_ple, w_pg16, w_ple16, i)
    return x2.reshape(batch, seq, D_MODEL)
```
